```python
import functools
import jax, jax.numpy as jnp
from jax import lax
import numpy as np

D_MODEL = 1024
BATCH = 2
SEQ = 8192
DEPTH = 4
DEC_BATCH = 128
DEC_SEQ = 4
PAST_LEN = 8192
PAGE_SIZE = 128

HEAD_DIM = 64
A_HEADS = D_MODEL // HEAD_DIM
KV_HEADS = A_HEADS // 4
GROUP = A_HEADS // KV_HEADS
ATTN_WIDTH = A_HEADS * HEAD_DIM
KV_WIDTH = KV_HEADS * HEAD_DIM
WINDOW = 128
ROT_DIM = HEAD_DIM // 4
ROPE_THETA = 500000.0
B_KEY = 128
B_HEADS = D_MODEL // B_KEY
B_VAL = D_MODEL // B_HEADS
F_DIM = B_HEADS * B_KEY
V_DIM = B_HEADS * B_VAL
CHUNK = 64
IN_DIM = ATTN_WIDTH + 2 * KV_WIDTH + 2 * F_DIM + 2 * V_DIM + 2 * D_MODEL
N_EXPERTS = 32
TOP_K = 4
D_FF = D_MODEL
SWIGLU_ALPHA = 1.702
SWIGLU_LIMIT = 7.0
MOE_BLOCK = 128
NORM_EPS = 1e-5

kernel_name = 'hybrid_swa_sink_hgrn2_moe_step'


def rms_norm(x, g):
    xf = x.astype(jnp.float32)
    y = xf * lax.rsqrt(jnp.mean(xf * xf, axis=-1, keepdims=True) + NORM_EPS)
    return (y * g.astype(jnp.float32)).astype(x.dtype)


def partial_rope(x, pos):
    half = ROT_DIM // 2
    inv_freq = ROPE_THETA ** (-2.0 * jnp.arange(half, dtype=jnp.float32) / ROT_DIM)
    ang = pos.astype(jnp.float32)[:, None] * inv_freq[None, :]
    cos = jnp.cos(ang)[:, None, :]
    sin = jnp.sin(ang)[:, None, :]
    xr = x[..., :ROT_DIM].astype(jnp.float32)
    x1, x2 = xr[..., :half], xr[..., half:]
    rot = jnp.concatenate([x1 * cos - x2 * sin, x2 * cos + x1 * sin], axis=-1).astype(x.dtype)
    return jnp.concatenate([rot, x[..., ROT_DIM:]], axis=-1)


def window_attention(q, k, v, q_pos, k_pos, sinks):
    s = jnp.einsum('bnqhgd,bnkhd->bnhgqk', q, k, preferred_element_type=jnp.float32) * (HEAD_DIM ** -0.5)
    qp = q_pos[:, :, None]
    kp = k_pos[:, None, :]
    valid = (kp <= qp) & (kp >= qp - WINDOW) & (kp >= 0)
    s = jnp.where(valid[None, :, None, None], s, -jnp.inf)
    sink_col = jnp.broadcast_to(sinks.astype(jnp.float32)[:, :, None, None], s.shape[:-1] + (1,))
    p = jax.nn.softmax(jnp.concatenate([s, sink_col], axis=-1), axis=-1)[..., :-1]
    return jnp.einsum('bnhgqk,bnkhd->bnqhgd', p.astype(v.dtype), v)


def attend_prompt(q, k, v, sinks):
    bsz, seq = q.shape[0], q.shape[1]
    nb = seq // WINDOW
    qb = q.reshape(bsz, nb, WINDOW, KV_HEADS, GROUP, HEAD_DIM)

    def band(t):
        tp = jnp.pad(t, ((0, 0), (WINDOW, 0), (0, 0), (0, 0))).reshape(bsz, nb + 1, WINDOW, KV_HEADS, HEAD_DIM)
        return jnp.concatenate([tp[:, :-1], tp[:, 1:]], axis=2)

    pos = jnp.arange(-WINDOW, seq, dtype=jnp.int32).reshape(nb + 1, WINDOW)
    k_pos = jnp.concatenate([pos[:-1], pos[1:]], axis=1)
    out = window_attention(qb, band(k), band(v), pos[1:], k_pos, sinks)
    return out.reshape(bsz, seq, ATTN_WIDTH), k[:, -WINDOW:], v[:, -WINDOW:]


def attend_sample(q, k, v, sinks, k_cache, v_cache):
    bsz, n_new = q.shape[0], q.shape[1]
    rows = k_cache.shape[1]
    kk = jnp.concatenate([k_cache.astype(k.dtype), k], axis=1)
    vv = jnp.concatenate([v_cache.astype(v.dtype), v], axis=1)
    q_pos = PAST_LEN + jnp.arange(n_new, dtype=jnp.int32)
    k_pos = jnp.concatenate([PAST_LEN - rows + jnp.arange(rows, dtype=jnp.int32), q_pos])
    out = window_attention(q.reshape(bsz, 1, n_new, KV_HEADS, GROUP, HEAD_DIM), kk[:, None], vv[:, None],
                           q_pos[None], k_pos[None], sinks)
    return out.reshape(bsz, n_new, ATTN_WIDTH), kk[:, -rows:], vv[:, -rows:]


def hgrn2_recurrence(q, k, v, log_f, s0):
    bsz, seq, nh, _ = q.shape
    c = CHUNK if seq % CHUNK == 0 else seq
    n = seq // c

    def to_chunks(t):
        return t.reshape(bsz, n, c, nh, t.shape[-1]).transpose(1, 0, 3, 2, 4)

    causal = jnp.arange(c)[:, None] >= jnp.arange(c)[None, :]

    def step(state, inp):
        qc, kc, vc, gc = inp
        g = jnp.cumsum(gc, axis=2)
        o_inter = jnp.einsum('bhtk,bhkv->bhtv', qc * jnp.exp(g), state)
        diff = jnp.where(causal[:, :, None], g[:, :, :, None, :] - g[:, :, None, :, :], -jnp.inf)
        scores = jnp.sum(qc[:, :, :, None, :] * kc[:, :, None, :, :] * jnp.exp(diff), axis=-1)
        o_intra = jnp.einsum('bhts,bhsv->bhtv', scores, vc)
        g_end = g[:, :, -1:, :]
        state = jnp.exp(g_end[:, :, 0, :])[..., None] * state + jnp.einsum('bhsk,bhsv->bhkv', kc * jnp.exp(g_end - g), vc)
        return state, o_inter + o_intra

    s_final, o = lax.scan(step, s0, (to_chunks(q), to_chunks(k), to_chunks(v), to_chunks(log_f)))
    return o.transpose(1, 0, 3, 2, 4).reshape(bsz, seq, nh, v.shape[-1]), s_final


def clamped_swiglu(h):
    glu = jnp.minimum(h[..., ::2], SWIGLU_LIMIT)
    lin = jnp.clip(h[..., 1::2], -SWIGLU_LIMIT, SWIGLU_LIMIT)
    return glu * jax.nn.sigmoid(SWIGLU_ALPHA * glu) * (lin + 1.0)


def moe_ffn(h, w_router, b_router, w_up, b_up, w_down, b_down):
    lead = h.shape[:-1]
    xf = h.reshape(-1, D_MODEL)
    n_tok = xf.shape[0]
    n_asg = n_tok * TOP_K
    logits = jnp.matmul(xf, w_router, preferred_element_type=jnp.float32) + b_router.astype(jnp.float32)
    top_val, top_idx = lax.top_k(logits, TOP_K)
    gate = jax.nn.softmax(top_val, axis=-1).reshape(-1)
    expert = top_idx.reshape(-1)
    token = jnp.arange(n_asg, dtype=jnp.int32) // TOP_K
    order = jnp.argsort(expert)
    s_exp, s_tok, s_gate = expert[order], token[order], gate[order]
    counts = jnp.bincount(expert, length=N_EXPERTS)
    padded = (counts + MOE_BLOCK - 1) // MOE_BLOCK * MOE_BLOCK
    pad_end = jnp.cumsum(padded)
    pad_start = pad_end - padded
    grp_start = jnp.cumsum(counts) - counts
    dest = pad_start[s_exp] + jnp.arange(n_asg, dtype=jnp.int32) - grp_start[s_exp]
    n_blocks = (n_asg + N_EXPERTS * (MOE_BLOCK - 1) + MOE_BLOCK - 1) // MOE_BLOCK
    buf = jnp.zeros((n_blocks * MOE_BLOCK, D_MODEL), h.dtype).at[dest].set(xf[s_tok])
    blk_expert = jnp.minimum(jnp.searchsorted(pad_end, jnp.arange(n_blocks, dtype=jnp.int32) * MOE_BLOCK, side='right'),
                             N_EXPERTS - 1)

    def expert_block(args):
        xb, e = args
        hb = xb @ w_up[e] + b_up[e]
        return clamped_swiglu(hb) @ w_down[e] + b_down[e]

    out = lax.map(expert_block, (buf.reshape(n_blocks, MOE_BLOCK, D_MODEL), blk_expert)).reshape(-1, D_MODEL)
    y = jax.ops.segment_sum(out[dest] * s_gate[:, None].astype(out.dtype), s_tok, num_segments=n_tok)
    return y.reshape(lead + (D_MODEL,)).astype(h.dtype)


def hybrid_layer(x, pos, attend, s0, norm_mix, w_in, sinks, lower_bound, hgrn_norm, w_out,
                 norm_ffn, w_router, b_router, w_up, b_up, w_down, b_down):
    bsz, seq, _ = x.shape
    h = rms_norm(x, norm_mix)
    z = h @ w_in
    cuts = np.cumsum([ATTN_WIDTH, KV_WIDTH, KV_WIDTH, F_DIM, F_DIM, V_DIM, V_DIM, D_MODEL]).tolist()
    qa, ka, va, qb, fb, ib, ogb, ga, gb = jnp.split(z, cuts, axis=-1)
    qa = partial_rope(qa.reshape(bsz, seq, A_HEADS, HEAD_DIM), pos)
    ka = partial_rope(ka.reshape(bsz, seq, KV_HEADS, HEAD_DIM), pos)
    va = va.reshape(bsz, seq, KV_HEADS, HEAD_DIM)
    a_out, k_keep, v_keep = attend(qa, ka, va, sinks.reshape(KV_HEADS, GROUP))
    lb = lower_bound.astype(jnp.float32)
    qh = (jax.nn.silu(qb.astype(jnp.float32)) * (B_KEY ** -0.5)).reshape(bsz, seq, B_HEADS, B_KEY)
    f = (lb + (1.0 - lb) * jax.nn.sigmoid(fb.astype(jnp.float32))).reshape(bsz, seq, B_HEADS, B_KEY)
    vi = ib.astype(jnp.float32).reshape(bsz, seq, B_HEADS, B_VAL)
    o, s_new = hgrn2_recurrence(qh, 1.0 - f, vi, jnp.log(f), s0)
    o = rms_norm(o, hgrn_norm) * jax.nn.silu(ogb.astype(jnp.float32)).reshape(bsz, seq, B_HEADS, B_VAL)
    b_out = o.reshape(bsz, seq, V_DIM).astype(x.dtype)
    merged = jax.nn.sigmoid(ga) * a_out + jax.nn.sigmoid(gb) * b_out
    x = x + merged @ w_out
    x = x + moe_ffn(rms_norm(x, norm_ffn), w_router, b_router, w_up, b_up, w_down, b_down)
    return x, k_keep, v_keep, s_new


def setup_inputs(seed: int = 0) -> dict:
    key = jax.random.key(seed)
    ks = jax.random.split(key, 19)

    def nrm(k, shape, scale):
        return jax.random.normal(k, shape, jnp.float32) * scale

    rows = min(WINDOW, PAST_LEN)
    return {
        'x_prompt': nrm(ks[0], (BATCH, SEQ, D_MODEL), 1.0),
        'x_sample': nrm(ks[1], (DEC_BATCH, DEC_SEQ, D_MODEL), 1.0),
        'cache_k': nrm(ks[2], (DEPTH, DEC_BATCH, rows, KV_HEADS, HEAD_DIM), 1.0),
        'cache_v': nrm(ks[3], (DEPTH, DEC_BATCH, rows, KV_HEADS, HEAD_DIM), 1.0),
        'state_hgrn': nrm(ks[4], (DEPTH, DEC_BATCH, B_HEADS, B_KEY, B_VAL), 0.3),
        'norm_mix': 1.0 + nrm(ks[5], (DEPTH, D_MODEL), 0.05),
        'w_in': nrm(ks[6], (DEPTH, D_MODEL, IN_DIM), D_MODEL ** -0.5),
        'attn_sinks': nrm(ks[7], (DEPTH, A_HEADS), 0.5),
        'lb_logits': nrm(ks[8], (DEPTH, F_DIM), 0.1),
        'hgrn_norm': 1.0 + nrm(ks[9], (DEPTH, B_VAL), 0.05),
        'w_out': nrm(ks[10], (DEPTH, D_MODEL, D_MODEL), 0.5 * D_MODEL ** -0.5),
        'norm_ffn': 1.0 + nrm(ks[11], (DEPTH, D_MODEL), 0.05),
        'w_router': nrm(ks[12], (DEPTH, D_MODEL, N_EXPERTS), D_MODEL ** -0.5),
        'b_router': nrm(ks[13], (DEPTH, N_EXPERTS), 0.01),
        'w_up': nrm(ks[14], (DEPTH, N_EXPERTS, D_MODEL, 2 * D_FF), D_MODEL ** -0.5),
        'b_up': nrm(ks[15], (DEPTH, N_EXPERTS, 2 * D_FF), 0.01),
        'w_down': nrm(ks[16], (DEPTH, N_EXPERTS, D_FF, D_MODEL), 0.5 * D_FF ** -0.5),
        'b_down': nrm(ks[17], (DEPTH, N_EXPERTS, D_MODEL), 0.01),
        'norm_final': 1.0 + nrm(ks[18], (D_MODEL,), 0.05),
    }


def reference(x_prompt, x_sample, cache_k, cache_v, state_hgrn, norm_mix, w_in, attn_sinks, lb_logits,
              hgrn_norm, w_out, norm_ffn, w_router, b_router, w_up, b_up, w_down, b_down, norm_final):
    lb_soft = jax.nn.softmax(lb_logits.astype(jnp.float32), axis=0)
    lower_bounds = jnp.cumsum(lb_soft, axis=0) - lb_soft[0:1]
    pos_p = jnp.arange(x_prompt.shape[1], dtype=jnp.int32)
    pos_s = PAST_LEN + jnp.arange(x_sample.shape[1], dtype=jnp.int32)
    s0_p = jnp.zeros((x_prompt.shape[0], B_HEADS, B_KEY, B_VAL), jnp.float32)
    xp, xs = x_prompt, x_sample
    kp_l, vp_l, sp_l, ks_l, vs_l, ss_l = [], [], [], [], [], []
    for l in range(DEPTH):
        w = (norm_mix[l], w_in[l], attn_sinks[l], lower_bounds[l], hgrn_norm[l], w_out[l], norm_ffn[l],
             w_router[l], b_router[l], w_up[l], b_up[l], w_down[l], b_down[l])
        xp, kp, vp, sp = hybrid_layer(xp, pos_p, attend_prompt, s0_p, *w)
        attend_s = functools.partial(attend_sample, k_cache=cache_k[l], v_cache=cache_v[l])
        xs, kx, vx, sx = hybrid_layer(xs, pos_s, attend_s, state_hgrn[l].astype(jnp.float32), *w)
        kp_l.append(kp); vp_l.append(vp); sp_l.append(sp)
        ks_l.append(kx); vs_l.append(vx); ss_l.append(sx)
    y_prompt = rms_norm(xp, norm_final)
    y_sample = rms_norm(xs, norm_final)
    new_k_prompt = jnp.stack(kp_l)
    new_v_prompt = jnp.stack(vp_l)
    new_state_prompt = jnp.stack(sp_l).astype(x_prompt.dtype)
    new_k_sample = jnp.stack(ks_l)
    new_v_sample = jnp.stack(vs_l)
    new_state_sample = jnp.stack(ss_l).astype(state_hgrn.dtype)
    return (y_prompt, y_sample, new_k_prompt, new_v_prompt, new_state_prompt, new_k_sample, new_v_sample, new_state_sample)
```

```python
import functools

import numpy as np
import jax
import jax.numpy as jnp
from jax import lax
from jax.experimental import pallas as pl
from jax.experimental.pallas import tpu as pltpu

F32 = jnp.float32
BF16 = jnp.bfloat16
I32 = jnp.int32

D_MODEL = 1024
HEAD_DIM = 64
A_HEADS = 16
KV_HEADS = 4
GROUP = 4
KV_WIDTH = KV_HEADS * HEAD_DIM
WINDOW = 128
PAST_LEN = 8192
ROT_DIM = 16
ROPE_THETA = 500000.0
B_KEY = 128
B_VAL = 128
B_HEADS = 8
CHUNK = 64
N_EXPERTS = 32
TOP_K = 4
D_FF = 1024
SWIGLU_ALPHA = 1.702
SWIGLU_LIMIT = 7.0
NORM_EPS = 1e-5
LANES = 128

SEG_QA, SEG_QB, SEG_FB, SEG_IB, SEG_OG, SEG_GA, SEG_GB = range(7)
KV_BLK_K = 7 * D_MODEL // KV_WIDTH
KV_BLK_V = KV_BLK_K + 1
IN_DIM = 7 * D_MODEL + 2 * KV_WIDTH

ROW_TILE_IN = 512
COL_TILE_IN = 1536
ROW_TILE_OUT = 256
ROW_TILE_ROUTE = 512
ROW_TILE_MOVE = 128
ROW_TILE_EXPERT = 256
SEQS_PER_STEP = 8

VMEM_LIMIT = 56 * 1024 * 1024


def _cparams(*sem):
    return pltpu.CompilerParams(dimension_semantics=sem, vmem_limit_bytes=VMEM_LIMIT)


def _tile(n, pref):
    t = pref
    while t > LANES and n % t:
        t //= 2
    assert n % t == 0, (n, pref)
    return t


def _sigmoid(x):
    return 1.0 / (1.0 + jnp.exp(-x))


def _silu(x):
    return x * _sigmoid(x)


def _inproj_kernel(x_ref, g_ref, w_ref, z_ref, h_scr):
    @pl.when(pl.program_id(1) == 0)
    def _():
        x = x_ref[...]
        ms = jnp.mean(x * x, axis=-1, keepdims=True)
        h_scr[...] = ((x * lax.rsqrt(ms + NORM_EPS)) * g_ref[...]).astype(BF16)

    z_ref[...] = jnp.dot(h_scr[...], w_ref[...], preferred_element_type=F32)


def _inproj(x, gain, w_bf16):
    t = x.shape[0]
    tm, tn = _tile(t, ROW_TILE_IN), COL_TILE_IN
    return pl.pallas_call(
        _inproj_kernel,
        grid=(t // tm, IN_DIM // tn),
        in_specs=[
            pl.BlockSpec((tm, D_MODEL), lambda i, j: (i, 0)),
            pl.BlockSpec((1, D_MODEL), lambda i, j: (0, 0)),
            pl.BlockSpec((D_MODEL, tn), lambda i, j: (0, j)),
        ],
        out_specs=pl.BlockSpec((tm, tn), lambda i, j: (i, j)),
        out_shape=jax.ShapeDtypeStruct((t, IN_DIM), F32),
        scratch_shapes=[pltpu.VMEM((tm, D_MODEL), BF16)],
        compiler_params=_cparams("arbitrary", "arbitrary"),
        name="inproj",
    )(x, gain.reshape(1, D_MODEL), w_bf16)


def _rope_tables(pos):
    half = ROT_DIM // 2
    inv_freq = ROPE_THETA ** (-2.0 * jnp.arange(half, dtype=F32) / ROT_DIM)
    ang = pos.astype(F32)[:, None] * inv_freq[None, :]
    cos, sin = jnp.cos(ang), jnp.sin(ang)
    n = pos.shape[0]
    rest = HEAD_DIM - ROT_DIM
    c_head = jnp.concatenate([cos, cos, jnp.ones((n, rest), F32)], axis=1)
    a_head = jnp.concatenate([-sin, jnp.zeros((n, half + rest), F32)], axis=1)
    b_head = jnp.concatenate([jnp.zeros((n, half), F32), sin, jnp.zeros((n, rest), F32)], axis=1)
    reps = LANES // HEAD_DIM
    return (jnp.tile(c_head, (1, reps)), jnp.tile(a_head, (1, reps)), jnp.tile(b_head, (1, reps)))


def _rope_slab(x, c, a, b):
    half = ROT_DIM // 2
    up = pltpu.roll(x, LANES - half, 1)
    dn = pltpu.roll(x, half, 1)
    return x * c + up * a + dn * b


def _rope(x, c, a, b):
    slabs = [_rope_slab(x[:, s * LANES:(s + 1) * LANES], c, a, b) for s in range(x.shape[1] // LANES)]
    return jnp.concatenate(slabs, axis=1)


def _attn_prompt_kernel(sink_ref, q_ref, k_ref, v_ref, c_ref, a_ref, b_ref,
                        o_ref, kn_ref, vn_ref, kprev, vprev):
    i = pl.program_id(1)
    w = WINDOW

    @pl.when(i == 0)
    def _():
        kprev[...] = jnp.zeros_like(kprev)
        vprev[...] = jnp.zeros_like(vprev)

    c, a, b = c_ref[...], a_ref[...], b_ref[...]
    q = (_rope(q_ref[...], c, a, b) * (HEAD_DIM ** -0.5)).astype(BF16)
    k_rot = _rope(k_ref[...], c, a, b)
    v_cur = v_ref[...]
    kn_ref[0] = k_rot
    vn_ref[0] = v_cur
    k_cur = k_rot.astype(BF16)
    v_curb = v_cur.astype(BF16)
    r = lax.broadcasted_iota(I32, (w, w), 0)
    col = lax.broadcasted_iota(I32, (w, w), 1)
    valid_p = (col >= jnp.where(i > 0, r, w))
    valid_c = col <= r
    nt = (((1,), (1,)), ((), ()))

    for h in range(A_HEADS):
        j = h // GROUP
        qh = q[:, h * HEAD_DIM:(h + 1) * HEAD_DIM]
        js = slice(j * HEAD_DIM, (j + 1) * HEAD_DIM)
        s_p = jnp.where(valid_p, lax.dot_general(qh, kprev[j], nt, preferred_element_type=F32), -jnp.inf)
        s_c = jnp.where(valid_c, lax.dot_general(qh, k_cur[:, js], nt, preferred_element_type=F32), -jnp.inf)
        sink = sink_ref[h]
        m = jnp.maximum(jnp.maximum(jnp.max(s_p, axis=-1, keepdims=True),
                                    jnp.max(s_c, axis=-1, keepdims=True)), sink)
        p_p = jnp.exp(s_p - m)
        p_c = jnp.exp(s_c - m)
        denom = (jnp.sum(p_p, axis=-1, keepdims=True) + jnp.sum(p_c, axis=-1, keepdims=True)
                 + jnp.exp(sink - m))
        inv = 1.0 / denom
        o_ref[:, h * HEAD_DIM:(h + 1) * HEAD_DIM] = (
            jnp.dot((p_p * inv).astype(BF16), vprev[j], preferred_element_type=F32)
            + jnp.dot((p_c * inv).astype(BF16), v_curb[:, js], preferred_element_type=F32))

    for j in range(KV_HEADS):
        kprev[j] = k_cur[:, j * HEAD_DIM:(j + 1) * HEAD_DIM]
        vprev[j] = v_curb[:, j * HEAD_DIM:(j + 1) * HEAD_DIM]


def _attn_prompt(z, sinks, tables, bsz, seq):
    nb = seq // WINDOW
    w = WINDOW
    row = lambda b, i: b * nb + i
    return pl.pallas_call(
        _attn_prompt_kernel,
        grid=(bsz, nb),
        in_specs=[
            pl.BlockSpec(memory_space=pltpu.SMEM),
            pl.BlockSpec((w, D_MODEL), lambda b, i: (row(b, i), SEG_QA)),
            pl.BlockSpec((w, KV_WIDTH), lambda b, i: (row(b, i), KV_BLK_K)),
            pl.BlockSpec((w, KV_WIDTH), lambda b, i: (row(b, i), KV_BLK_V)),
            pl.BlockSpec((w, LANES), lambda b, i: (i, 0)),
            pl.BlockSpec((w, LANES), lambda b, i: (i, 0)),
            pl.BlockSpec((w, LANES), lambda b, i: (i, 0)),
        ],
        out_specs=[
            pl.BlockSpec((w, D_MODEL), lambda b, i: (row(b, i), 0)),
            pl.BlockSpec((1, w, KV_WIDTH), lambda b, i: (b, 0, 0)),
            pl.BlockSpec((1, w, KV_WIDTH), lambda b, i: (b, 0, 0)),
        ],
        out_shape=[
            jax.ShapeDtypeStruct((bsz * seq, D_MODEL), F32),
            jax.ShapeDtypeStruct((bsz, w, KV_WIDTH), F32),
            jax.ShapeDtypeStruct((bsz, w, KV_WIDTH), F32),
        ],
        scratch_shapes=[pltpu.VMEM((KV_HEADS, w, HEAD_DIM), BF16), pltpu.VMEM((KV_HEADS, w, HEAD_DIM), BF16)],
        compiler_params=_cparams("arbitrary", "arbitrary"),
        name="attn_prompt",
    )(sinks, z, z, z, *tables)


def _attn_sample_kernel(sink_ref, q_ref, k_ref, v_ref, kc_ref, vc_ref, c_ref, a_ref, b_ref,
                        o_ref, kn_ref, vn_ref, *, n_new):
    rows = kc_ref.shape[1]
    c, a, b = c_ref[...], a_ref[...], b_ref[...]
    tq = lax.broadcasted_iota(I32, (n_new, rows), 0)
    cc = lax.broadcasted_iota(I32, (n_new, rows), 1)
    valid_c = cc >= tq
    tn = lax.broadcasted_iota(I32, (n_new, n_new), 0)
    nn = lax.broadcasted_iota(I32, (n_new, n_new), 1)
    valid_n = nn <= tn
    for sq in range(SEQS_PER_STEP):
        q = (_rope(q_ref[sq], c, a, b) * (HEAD_DIM ** -0.5)).astype(BF16)
        k_new = _rope(k_ref[sq], c, a, b)
        v_new = v_ref[sq]
        kc = kc_ref[sq]
        vc = vc_ref[sq]
        kn_ref[sq, 0:rows - n_new, :] = kc[n_new:, :]
        kn_ref[sq, rows - n_new:rows, :] = k_new
        vn_ref[sq, 0:rows - n_new, :] = vc[n_new:, :]
        vn_ref[sq, rows - n_new:rows, :] = v_new
        kcb, vcb = kc.astype(BF16), vc.astype(BF16)
        knb, vnb = k_new.astype(BF16), v_new.astype(BF16)
        outs = []
        for h in range(A_HEADS):
            j = h // GROUP
            hs = slice(h * HEAD_DIM, (h + 1) * HEAD_DIM)
            js = slice(j * HEAD_DIM, (j + 1) * HEAD_DIM)
            qh = q[:, hs]
            s_c = lax.dot_general(qh, kcb[:, js], (((1,), (1,)), ((), ())), preferred_element_type=F32)
            s_n = lax.dot_general(qh, knb[:, js], (((1,), (1,)), ((), ())), preferred_element_type=F32)
            s_c = jnp.where(valid_c, s_c, -jnp.inf)
            s_n = jnp.where(valid_n, s_n, -jnp.inf)
            sink = sink_ref[h]
            m = jnp.maximum(jnp.maximum(jnp.max(s_c, axis=-1, keepdims=True),
                                        jnp.max(s_n, axis=-1, keepdims=True)), sink)
            p_c = jnp.exp(s_c - m)
            p_n = jnp.exp(s_n - m)
            denom = (jnp.sum(p_c, axis=-1, keepdims=True) + jnp.sum(p_n, axis=-1, keepdims=True)
                     + jnp.exp(sink - m))
            inv = 1.0 / denom
            o = (jnp.dot((p_c * inv).astype(BF16), vcb[:, js], preferred_element_type=F32)
                 + jnp.dot((p_n * inv).astype(BF16), vnb[:, js], preferred_element_type=F32))
            outs.append(o)
        o_ref[sq] = jnp.concatenate(outs, axis=1)


def _attn_sample(zs, sinks, tables, cache_k, cache_v):
    n_seq, n_new, _ = zs.shape
    rows = cache_k.shape[1]
    g = SEQS_PER_STEP
    kern = functools.partial(_attn_sample_kernel, n_new=n_new)
    return pl.pallas_call(
        kern,
        grid=(n_seq // g,),
        in_specs=[
            pl.BlockSpec(memory_space=pltpu.SMEM),
            pl.BlockSpec((g, n_new, D_MODEL), lambda i: (i, 0, SEG_QA)),
            pl.BlockSpec((g, n_new, KV_WIDTH), lambda i: (i, 0, KV_BLK_K)),
            pl.BlockSpec((g, n_new, KV_WIDTH), lambda i: (i, 0, KV_BLK_V)),
            pl.BlockSpec((g, rows, KV_WIDTH), lambda i: (i, 0, 0)),
            pl.BlockSpec((g, rows, KV_WIDTH), lambda i: (i, 0, 0)),
            pl.BlockSpec((n_new, LANES), lambda i: (0, 0)),
            pl.BlockSpec((n_new, LANES), lambda i: (0, 0)),
            pl.BlockSpec((n_new, LANES), lambda i: (0, 0)),
        ],
        out_specs=[
            pl.BlockSpec((g, n_new, D_MODEL), lambda i: (i, 0, 0)),
            pl.BlockSpec((g, rows, KV_WIDTH), lambda i: (i, 0, 0)),
            pl.BlockSpec((g, rows, KV_WIDTH), lambda i: (i, 0, 0)),
        ],
        out_shape=[
            jax.ShapeDtypeStruct((n_seq, n_new, D_MODEL), F32),
            jax.ShapeDtypeStruct(cache_k.shape, F32),
            jax.ShapeDtypeStruct(cache_v.shape, F32),
        ],
        compiler_params=_cparams("arbitrary"),
        name="attn_sample",
    )(sinks, zs, zs, zs, cache_k, cache_v, *tables)


def _hgrn_levels(c):
    lv = []
    hb = c // 2
    while hb >= 1:
        lv.append(hb)
        hb //= 2
    return lv


def _hgrn_prompt_kernel(qb_ref, fb_ref, ib_ref, og_ref, lb_ref, hn_ref, tri_ref,
                        o_ref, st_ref, g_scr, state):
    ci = pl.program_id(1)
    nc = pl.num_programs(1)
    c = CHUNK

    @pl.when(ci == 0)
    def _():
        state[...] = jnp.zeros_like(state)

    lb = lb_ref[...]
    f = lb + (1.0 - lb) * _sigmoid(fb_ref[...])
    logf = jnp.log(f)
    g_scr[...] = jnp.dot(tri_ref[...], logf, preferred_element_type=F32, precision=lax.Precision.HIGHEST)

    rt = lax.broadcasted_iota(I32, (c, c), 0)
    cs = lax.broadcasted_iota(I32, (c, c), 1)
    masks = []
    for hb in _hgrn_levels(c):
        same = (rt // (2 * hb)) == (cs // (2 * hb))
        masks.append(same & ((rt // hb) % 2 == 1) & ((cs // hb) % 2 == 0))
    row_t = lax.broadcasted_iota(I32, (8, LANES), 0)

    for h in range(B_HEADS):
        hs = slice(h * B_KEY, (h + 1) * B_KEY)
        qh = _silu(qb_ref[:, hs]) * (B_KEY ** -0.5)
        fh = f[:, hs]
        kk = 1.0 - fh
        v = ib_ref[:, hs]
        vb = v.astype(BF16)
        g = g_scr[:, hs]
        scores = jnp.zeros((c, c), F32)
        def bcast_row(r):
            return jnp.broadcast_to(g_scr[pl.ds(r, 1), hs], (8, B_KEY))

        for lvl, hb in enumerate(_hgrn_levels(c)):
            tiles = []
            for tix in range(c // 8):
                base = tix * 8
                if hb >= 8:
                    tiles.append(bcast_row((base // (2 * hb)) * 2 * hb + hb - 1))
                else:
                    refs = sorted({(tt // (2 * hb)) * 2 * hb + hb - 1 for tt in range(8)})
                    cur = bcast_row(base + refs[0])
                    for rr in refs[1:]:
                        cur = jnp.where(row_t >= (rr - hb + 1), bcast_row(base + rr), cur)
                    tiles.append(cur)
            ref = jnp.concatenate(tiles, axis=0)
            e = jnp.exp(-jnp.abs(g - ref))
            qs = (qh * e).astype(BF16)
            ks = (kk * e).astype(BF16)
            part = lax.dot_general(qs, ks, (((1,), (1,)), ((), ())), preferred_element_type=F32)
            scores = scores + jnp.where(masks[lvl], part, 0.0)
        diag = jnp.sum(qh * kk, axis=-1, keepdims=True)
        o_intra = jnp.dot(scores.astype(BF16), vb, preferred_element_type=F32) + diag * v
        st = state[h]
        qg = (qh * jnp.exp(g)).astype(BF16)
        o_inter = lax.dot_general(qg, st.astype(BF16), (((1,), (1,)), ((), ())), preferred_element_type=F32)
        g_end = g[c - 1:c, :]
        kd = (kk * jnp.exp(g_end - g)).astype(BF16)
        upd = lax.dot_general(vb, kd, (((0,), (0,)), ((), ())), preferred_element_type=F32)
        state[h] = jnp.exp(g_end) * st + upd
        o = o_inter + o_intra
        ms = jnp.mean(o * o, axis=-1, keepdims=True)
        o = (o * lax.rsqrt(ms + NORM_EPS)) * hn_ref[...]
        o_ref[:, hs] = o * _silu(og_ref[:, hs])

    @pl.when(ci == nc - 1)
    def _():
        st_ref[0] = state[...]


def _tri(c):
    return jnp.asarray(np.tril(np.ones((c, c), np.float32)))


def _hgrn_prompt(z, lower_bound, hgrn_norm, bsz, seq):
    c = CHUNK
    nc = seq // c
    row = lambda b, i: b * nc + i
    seg = lambda s: pl.BlockSpec((c, D_MODEL), lambda b, i: (row(b, i), s))
    return pl.pallas_call(
        _hgrn_prompt_kernel,
        grid=(bsz, nc),
        in_specs=[
            seg(SEG_QB), seg(SEG_FB), seg(SEG_IB), seg(SEG_OG),
            pl.BlockSpec((1, D_MODEL), lambda b, i: (0, 0)),
            pl.BlockSpec((1, B_VAL), lambda b, i: (0, 0)),
            pl.BlockSpec((c, c), lambda b, i: (0, 0)),
        ],
        out_specs=[
            pl.BlockSpec((c, D_MODEL), lambda b, i: (row(b, i), 0)),
            pl.BlockSpec((1, B_HEADS, B_VAL, B_KEY), lambda b, i: (b, 0, 0, 0)),
        ],
        out_shape=[
            jax.ShapeDtypeStruct((bsz * seq, D_MODEL), F32),
            jax.ShapeDtypeStruct((bsz, B_HEADS, B_VAL, B_KEY), F32),
        ],
        scratch_shapes=[pltpu.VMEM((c, D_MODEL), F32), pltpu.VMEM((B_HEADS, B_VAL, B_KEY), F32)],
        compiler_params=_cparams("arbitrary", "arbitrary"),
        name="hgrn_prompt",
    )(z, z, z, z, lower_bound.reshape(1, D_MODEL), hgrn_norm.reshape(1, B_VAL), _tri(c))


def _hgrn_sample_kernel(qb_ref, fb_ref, ib_ref, og_ref, s_ref, lb_ref, hn_ref,
                        o_ref, sn_ref, *, n_new):
    lb = lb_ref[...]
    rows_t = lax.broadcasted_iota(I32, (n_new, 1), 0)
    for sq in range(SEQS_PER_STEP):
        f = lb + (1.0 - lb) * _sigmoid(fb_ref[sq])
        logf = jnp.log(f)
        g_rows = [logf[0:1, :]]
        for t in range(1, n_new):
            g_rows.append(g_rows[-1] + logf[t:t + 1, :])
        g_all = jnp.concatenate(g_rows, axis=0)
        qh_all = _silu(qb_ref[sq]) * (B_KEY ** -0.5)
        kk_all = 1.0 - f
        v_all = ib_ref[sq]
        og_all = og_ref[sq]
        outs = []
        for h in range(B_HEADS):
            hs = slice(h * B_KEY, (h + 1) * B_KEY)
            g, qh, kk, v = g_all[:, hs], qh_all[:, hs], kk_all[:, hs], v_all[:, hs]
            s0 = s_ref[sq, h]
            o = jnp.dot((qh * jnp.exp(g)).astype(BF16), s0.astype(BF16), preferred_element_type=F32)
            for s in range(n_new):
                diff = jnp.where(rows_t >= s, g - g[s:s + 1, :], 0.0)
                sc = jnp.sum(qh * kk[s:s + 1, :] * jnp.exp(diff), axis=-1, keepdims=True)
                sc = jnp.where(rows_t >= s, sc, 0.0)
                o = o + sc * v[s:s + 1, :]
            g_end = g[n_new - 1:n_new, :]
            kd = (kk * jnp.exp(g_end - g)).astype(BF16)
            upd = lax.dot_general(kd, v.astype(BF16), (((0,), (0,)), ((), ())), preferred_element_type=F32)
            decay = jnp.transpose(jnp.broadcast_to(jnp.exp(g_end), (B_VAL, B_KEY)))
            sn_ref[sq, h] = decay * s0 + upd
            ms = jnp.mean(o * o, axis=-1, keepdims=True)
            o = (o * lax.rsqrt(ms + NORM_EPS)) * hn_ref[...]
            outs.append(o * _silu(og_all[:, hs]))
        o_ref[sq] = jnp.concatenate(outs, axis=1)


def _hgrn_sample(zs, state, lower_bound, hgrn_norm):
    n_seq, n_new, _ = zs.shape
    g = SEQS_PER_STEP
    seg = lambda s: pl.BlockSpec((g, n_new, D_MODEL), lambda i: (i, 0, s))
    st_spec = pl.BlockSpec((g, B_HEADS, B_KEY, B_VAL), lambda i: (i, 0, 0, 0))
    kern = functools.partial(_hgrn_sample_kernel, n_new=n_new)
    return pl.pallas_call(
        kern,
        grid=(n_seq // g,),
        in_specs=[
            seg(SEG_QB), seg(SEG_FB), seg(SEG_IB), seg(SEG_OG), st_spec,
            pl.BlockSpec((1, D_MODEL), lambda i: (0, 0)),
            pl.BlockSpec((1, B_VAL), lambda i: (0, 0)),
        ],
        out_specs=[pl.BlockSpec((g, n_new, D_MODEL), lambda i: (i, 0, 0)), st_spec],
        out_shape=[jax.ShapeDtypeStruct((n_seq, n_new, D_MODEL), F32), jax.ShapeDtypeStruct(state.shape, F32)],
        compiler_params=_cparams("arbitrary"),
        name="hgrn_sample",
    )(zs, zs, zs, zs, state, lower_bound.reshape(1, D_MODEL), hgrn_norm.reshape(1, B_VAL))


def _merge_kernel(ga_ref, gb_ref, ap_ref, as_ref, bp_ref, bs_ref, x_ref, wo_ref, gn_ref, wr_ref, br_ref,
                  xn_ref, h_ref, lg_ref, *, prompt_tiles):
    is_prompt = pl.program_id(0) < prompt_tiles
    a = jnp.where(is_prompt, ap_ref[...], as_ref[...])
    b = jnp.where(is_prompt, bp_ref[...], bs_ref[...])
    merged = _sigmoid(ga_ref[...]) * a + _sigmoid(gb_ref[...]) * b
    x = x_ref[...] + jnp.dot(merged.astype(BF16), wo_ref[...], preferred_element_type=F32)
    xn_ref[...] = x
    ms = jnp.mean(x * x, axis=-1, keepdims=True)
    h = (x * lax.rsqrt(ms + NORM_EPS)) * gn_ref[...]
    h_ref[...] = h
    lg_ref[...] = jnp.dot(h.astype(BF16), wr_ref[...], preferred_element_type=F32) + br_ref[...]


def _merge(z, a_p, a_s, b_p, b_s, x, w_out_bf16, norm_ffn, w_router_pad, b_router_pad):
    t = x.shape[0]
    tp, ts = a_p.shape[0], a_s.shape[0]
    tm = _tile(np.gcd(tp, ts), ROW_TILE_OUT)
    npt = tp // tm
    rowspec = lambda w, cb: pl.BlockSpec((tm, w), lambda i: (i, cb))
    pspec = pl.BlockSpec((tm, D_MODEL), lambda i: (jnp.minimum(i, npt - 1), 0))
    sspec = pl.BlockSpec((tm, D_MODEL), lambda i: (jnp.maximum(i - npt, 0), 0))
    const = lambda shape: pl.BlockSpec(shape, lambda i: (0, 0))
    return pl.pallas_call(
        functools.partial(_merge_kernel, prompt_tiles=npt),
        grid=(t // tm,),
        in_specs=[
            rowspec(D_MODEL, SEG_GA), rowspec(D_MODEL, SEG_GB),
            pspec, sspec, pspec, sspec, rowspec(D_MODEL, 0),
            const((D_MODEL, D_MODEL)), const((1, D_MODEL)), const((D_MODEL, LANES)), const((1, LANES)),
        ],
        out_specs=[rowspec(D_MODEL, 0), rowspec(D_MODEL, 0), rowspec(LANES, 0)],
        out_shape=[
            jax.ShapeDtypeStruct((t, D_MODEL), F32),
            jax.ShapeDtypeStruct((t, D_MODEL), F32),
            jax.ShapeDtypeStruct((t, LANES), F32),
        ],
        compiler_params=_cparams("arbitrary"),
        name="merge_outproj",
    )(z, z, a_p, a_s, b_p, b_s, x, w_out_bf16, norm_ffn.reshape(1, D_MODEL), w_router_pad, b_router_pad)


def _route_kernel(lg_ref, tri_ref, eidx_ref, gate_ref, rank_ref, cnt_ref, carry):
    @pl.when(pl.program_id(0) == 0)
    def _():
        carry[...] = jnp.zeros_like(carry)

    tm = lg_ref.shape[0]
    lane = lax.broadcasted_iota(I32, (tm, LANES), 1)
    lane_f = lane.astype(F32)
    l = jnp.where(lane < N_EXPERTS, lg_ref[...], -jnp.inf)
    vals, idxs = [], []
    picked = jnp.zeros((tm, LANES), F32)
    for _ in range(TOP_K):
        m = jnp.max(l, axis=-1, keepdims=True)
        idx = jnp.min(jnp.where(l == m, lane_f, float(LANES)), axis=-1, keepdims=True).astype(I32)
        sel = lane == idx
        vals.append(m)
        idxs.append(idx)
        picked = picked + sel.astype(F32)
        l = jnp.where(sel, -jnp.inf, l)
    exps = [jnp.exp(v - vals[0]) for v in vals]
    total = exps[0]
    for e in exps[1:]:
        total = total + e
    inv = 1.0 / total
    before = jnp.dot(tri_ref[...], picked.astype(BF16), preferred_element_type=F32) + carry[...]
    eidx = jnp.zeros((tm, LANES), I32)
    gate = jnp.zeros((tm, LANES), F32)
    rank = jnp.zeros((tm, LANES), I32)
    for r in range(TOP_K):
        rk = jnp.sum(jnp.where(lane == idxs[r], before, 0.0), axis=-1, keepdims=True)
        eidx = jnp.where(lane == r, idxs[r], eidx)
        gate = jnp.where(lane == r, exps[r] * inv, gate)
        rank = jnp.where(lane == r, rk.astype(I32), rank)
    eidx_ref[...] = eidx
    gate_ref[...] = gate
    rank_ref[...] = rank
    carry[...] = carry[...] + jnp.sum(picked, axis=0, keepdims=True)
    cnt_ref[...] = carry[...]


def _route(logits):
    t = logits.shape[0]
    tm = _tile(t, ROW_TILE_ROUTE)
    tri =jnp.asarray(np.tril(np.ones((tm, tm), np.float32), -1)).astype(BF16)
    rows = pl.BlockSpec((tm, LANES), lambda i: (i, 0))
    return pl.pallas_call(
        _route_kernel,
        grid=(t // tm,),
        in_specs=[rows, pl.BlockSpec((tm, tm), lambda i: (0, 0))],
        out_specs=[rows, rows, rows, pl.BlockSpec((1, LANES), lambda i: (0, 0))],
        out_shape=[
            jax.ShapeDtypeStruct((t, LANES), I32),
            jax.ShapeDtypeStruct((t, LANES), F32),
            jax.ShapeDtypeStruct((t, LANES), I32),
            jax.ShapeDtypeStruct((1, LANES), F32),
        ],
        scratch_shapes=[pltpu.VMEM((1, LANES), F32)],
        compiler_params=_cparams("arbitrary"),
        name="route",
    )(logits, tri)


def _scatter_kernel(dest_ref, h_hbm, xs_in, xs_hbm, sem):
    del xs_in
    i = pl.program_id(0)
    n = pl.num_programs(0)
    tm = ROW_TILE_MOVE

    def row_copy(tok, d, slot):
        return pltpu.make_async_copy(h_hbm.at[pl.ds(tok, 1)], xs_hbm.at[pl.ds(d, 1)], sem.at[slot])

    def issue(r, carry):
        tok = i * tm + r
        for k in range(TOP_K):
            row_copy(tok, dest_ref[r * TOP_K + k], i % 2).start()
        return carry

    lax.fori_loop(0, tm, issue, 0)

    def drain(slot):
        def body(r, carry):
            for _ in range(TOP_K):
                row_copy(0, 0, slot).wait()
            return carry
        lax.fori_loop(0, tm, body, 0)

    @pl.when(i > 0)
    def _():
        drain((i + 1) % 2)

    @pl.when(i == n - 1)
    def _():
        drain(i % 2)


def _scatter_rows(h, dest_flat, xs_zero):
    t = h.shape[0]
    tm = ROW_TILE_MOVE
    return pl.pallas_call(
        _scatter_kernel,
        grid=(t // tm,),
        in_specs=[
            pl.BlockSpec((tm * TOP_K,), lambda i: (i,), memory_space=pltpu.SMEM),
            pl.BlockSpec(memory_space=pl.ANY),
            pl.BlockSpec(memory_space=pl.ANY),
        ],
        out_specs=pl.BlockSpec(memory_space=pl.ANY),
        out_shape=jax.ShapeDtypeStruct(xs_zero.shape, xs_zero.dtype),
        scratch_shapes=[pltpu.SemaphoreType.DMA((2,))],
        input_output_aliases={2: 0},
        compiler_params=_cparams("arbitrary"),
        name="scatter_rows",
    )(dest_flat, h, xs_zero)


def _expert_kernel(blk_ref, nv_ref, xs_ref, wu_ref, bu_ref, wd_ref, bd_ref, o_ref):
    @pl.when(pl.program_id(0) >= nv_ref[0])
    def _():
        o_ref[...] = jnp.zeros_like(o_ref)

    @pl.when(pl.program_id(0) < nv_ref[0])
    def _():
        x = xs_ref[...].astype(BF16)
        h = jnp.dot(x, wu_ref[0], preferred_element_type=F32) + bu_ref[0]
        glu = jnp.minimum(h[:, :D_FF], SWIGLU_LIMIT)
        lin = jnp.clip(h[:, D_FF:], -SWIGLU_LIMIT, SWIGLU_LIMIT)
        act = glu * _sigmoid(SWIGLU_ALPHA * glu) * (lin + 1.0)
        o_ref[...] = jnp.dot(act.astype(BF16), wd_ref[0], preferred_element_type=F32) + bd_ref[0]


def _experts(xs, blk_expert, n_valid, w_up, b_up, w_down, b_down):
    rows = xs.shape[0]
    tm = ROW_TILE_EXPERT
    tile = lambda i, blk, nv: (jnp.minimum(i, nv[0] - 1), 0)
    wsel = lambda i, blk, nv: (blk[i], 0, 0)
    grid_spec = pltpu.PrefetchScalarGridSpec(
        num_scalar_prefetch=2,
        grid=(rows // tm,),
        in_specs=[
            pl.BlockSpec((tm, D_MODEL), tile),
            pl.BlockSpec((1, D_MODEL, 2 * D_FF), wsel),
            pl.BlockSpec((1, 1, 2 * D_FF), wsel),
            pl.BlockSpec((1, D_FF, D_MODEL), wsel),
            pl.BlockSpec((1, 1, D_MODEL), wsel),
        ],
        out_specs=pl.BlockSpec((tm, D_MODEL), lambda i, blk, nv: (i, 0)),
    )
    return pl.pallas_call(
        _expert_kernel,
        grid_spec=grid_spec,
        out_shape=jax.ShapeDtypeStruct((rows, D_MODEL), F32),
        compiler_params=_cparams("arbitrary"),
        name="experts",
    )(blk_expert, n_valid, xs, w_up, b_up, w_down, b_down)


def _combine_kernel(dest_ref, dnext_ref, gate_ref, x_ref, gn_ref, ys_hbm, o_ref, buf, sem, *, final_norm):
    i = pl.program_id(0)
    n = pl.num_programs(0)
    tm = ROW_TILE_MOVE

    def row_copy(d, k, r, slot):
        return pltpu.make_async_copy(ys_hbm.at[pl.ds(d, 1)], buf.at[slot, k, pl.ds(r, 1)], sem.at[slot])

    def issue(idx_ref, slot):
        def body(r, carry):
            for k in range(TOP_K):
                row_copy(idx_ref[r * TOP_K + k], k, r, slot).start()
            return carry
        lax.fori_loop(0, tm, body, 0)

    @pl.when(i == 0)
    def _():
        issue(dest_ref, 0)

    @pl.when(i + 1 < n)
    def _():
        issue(dnext_ref, (i + 1) % 2)

    slot = i % 2

    def drain(r, carry):
        for k in range(TOP_K):
            row_copy(0, k, r, slot).wait()
        return carry

    lax.fori_loop(0, tm, drain, 0)

    gate = gate_ref[...]
    y = x_ref[...]
    for k in range(TOP_K):
        y = y + gate[:, k:k + 1] * buf[slot, k]
    if final_norm:
        ms = jnp.mean(y * y, axis=-1, keepdims=True)
        y = (y * lax.rsqrt(ms + NORM_EPS)) * gn_ref[...]
    o_ref[...] = y


def _combine(x, ys, dest_flat, gates, gain, final_norm):
    t = x.shape[0]
    tm = ROW_TILE_MOVE
    nt = t // tm
    kern = functools.partial(_combine_kernel, final_norm=final_norm)
    return pl.pallas_call(
        kern,
        grid=(nt,),
        in_specs=[
            pl.BlockSpec((tm * TOP_K,), lambda i: (i,), memory_space=pltpu.SMEM),
            pl.BlockSpec((tm * TOP_K,), lambda i: (jnp.minimum(i + 1, nt - 1),), memory_space=pltpu.SMEM),
            pl.BlockSpec((tm, LANES), lambda i: (i, 0)),
            pl.BlockSpec((tm, D_MODEL), lambda i: (i, 0)),
            pl.BlockSpec((1, D_MODEL), lambda i: (0, 0)),
            pl.BlockSpec(memory_space=pl.ANY),
        ],
        out_specs=pl.BlockSpec((tm, D_MODEL), lambda i: (i, 0)),
        out_shape=jax.ShapeDtypeStruct((t, D_MODEL), F32),
        scratch_shapes=[pltpu.VMEM((2, TOP_K, tm, D_MODEL), F32), pltpu.SemaphoreType.DMA((2,))],
        compiler_params=_cparams("arbitrary"),
        name="combine",
    )(dest_flat, dest_flat, gates, x, gain.reshape(1, D_MODEL), ys)


def _moe(x, h, logits, w_up, b_up, w_down, b_down, gain, final_norm):
    t = x.shape[0]
    tm = ROW_TILE_EXPERT
    eidx, gates, rank, counts = _route(logits)
    counts = counts[0, :N_EXPERTS].astype(I32)
    padded = (counts + tm - 1) // tm * tm
    pad_end = jnp.cumsum(padded)
    pad_start = pad_end - padded
    dest = (pad_start[eidx[:, :TOP_K]] + rank[:, :TOP_K]).reshape(-1)
    n_tiles = (t * TOP_K + N_EXPERTS * (tm - 1) + tm - 1) // tm
    blk_expert = jnp.minimum(
        jnp.searchsorted(pad_end, jnp.arange(n_tiles, dtype=I32) * tm, side="right"), N_EXPERTS - 1).astype(I32)
    n_valid = (pad_end[-1:] // tm).astype(I32)
    xs = _scatter_rows(h, dest, jnp.zeros((n_tiles * tm, D_MODEL), F32))
    ys = _experts(xs, blk_expert, n_valid, w_up, b_up, w_down, b_down)
    return _combine(x, ys, dest, gates, gain, final_norm)


def kernel(x_prompt, x_sample, cache_k, cache_v, state_hgrn, norm_mix, w_in, attn_sinks, lb_logits,
           hgrn_norm, w_out, norm_ffn, w_router, b_router, w_up, b_up, w_down, b_down, norm_final):
    bsz, seq, _ = x_prompt.shape
    n_seq, n_new, _ = x_sample.shape
    depth = w_in.shape[0]
    rows = cache_k.shape[2]
    tp = bsz * seq

    lb_soft = jax.nn.softmax(lb_logits.astype(F32), axis=0)
    lower_bounds = jnp.cumsum(lb_soft, axis=0) - lb_soft[0:1]

    cuts = np.cumsum([0, D_MODEL, KV_WIDTH, KV_WIDTH, D_MODEL, D_MODEL, D_MODEL, D_MODEL, D_MODEL, D_MODEL])
    part = lambda s: w_in[:, :, cuts[s]:cuts[s + 1]]
    w_in_p = jnp.concatenate([part(0), part(3), part(4), part(5), part(6), part(7), part(8), part(1), part(2)],
                             axis=-1).astype(BF16)
    w_out_b = w_out.astype(BF16)
    w_up_p = jnp.concatenate([w_up[..., 0::2], w_up[..., 1::2]], axis=-1).astype(BF16)
    b_up_p = jnp.concatenate([b_up[..., 0::2], b_up[..., 1::2]], axis=-1).reshape(depth, N_EXPERTS, 1, 2 * D_FF)
    w_down_b = w_down.astype(BF16)
    b_down_r = b_down.reshape(depth, N_EXPERTS, 1, D_MODEL)
    w_router_p = jnp.pad(w_router, ((0, 0), (0, 0), (0, LANES - N_EXPERTS))).astype(BF16)
    b_router_p = jnp.pad(b_router, ((0, 0), (0, LANES - N_EXPERTS))).reshape(depth, 1, LANES)

    tab_p = _rope_tables(jnp.arange(seq, dtype=I32))
    tab_s = _rope_tables(PAST_LEN + jnp.arange(n_new, dtype=I32))

    x = jnp.concatenate([x_prompt.reshape(tp, D_MODEL), x_sample.reshape(n_seq * n_new, D_MODEL)], axis=0)
    kp_l, vp_l, sp_l, ks_l, vs_l, ss_l = [], [], [], [], [], []
    for l in range(depth):
        z = _inproj(x, norm_mix[l], w_in_p[l])
        zs = z[tp:].reshape(n_seq, n_new, IN_DIM)
        a_p, kp, vp = _attn_prompt(z, attn_sinks[l], tab_p, bsz, seq)
        a_s, kx, vx = _attn_sample(zs, attn_sinks[l], tab_s,
                                   cache_k[l].reshape(n_seq, rows, KV_WIDTH),
                                   cache_v[l].reshape(n_seq, rows, KV_WIDTH))
        b_p, sp = _hgrn_prompt(z, lower_bounds[l], hgrn_norm[l], bsz, seq)
        b_s, sx = _hgrn_sample(zs, state_hgrn[l].astype(F32), lower_bounds[l], hgrn_norm[l])
        x_mid, h, logits = _merge(z, a_p, a_s.reshape(-1, D_MODEL), b_p, b_s.reshape(-1, D_MODEL), x,
                                  w_out_b[l], norm_ffn[l], w_router_p[l], b_router_p[l])
        last = l == depth - 1
        x = _moe(x_mid, h, logits, w_up_p[l], b_up_p[l], w_down_b[l], b_down_r[l], norm_final, last)
        kp_l.append(kp.reshape(bsz, WINDOW, KV_HEADS, HEAD_DIM))
        vp_l.append(vp.reshape(bsz, WINDOW, KV_HEADS, HEAD_DIM))
        sp_l.append(jnp.swapaxes(sp, -1, -2))
        ks_l.append(kx.reshape(n_seq, rows, KV_HEADS, HEAD_DIM))
        vs_l.append(vx.reshape(n_seq, rows, KV_HEADS, HEAD_DIM))
        ss_l.append(sx)
    y_prompt = x[:tp].reshape(bsz, seq, D_MODEL)
    y_sample = x[tp:].reshape(n_seq, n_new, D_MODEL)
    return (y_prompt, y_sample, jnp.stack(kp_l), jnp.stack(vp_l), jnp.stack(sp_l).astype(x_prompt.dtype),
            jnp.stack(ks_l), jnp.stack(vs_l), jnp.stack(ss_l).astype(state_hgrn.dtype))
```

```python
import functools

import numpy as np
import jax
import jax.numpy as jnp
from jax import lax
from jax.experimental import pallas as pl
from jax.experimental.pallas import tpu as pltpu

F32 = jnp.float32
BF16 = jnp.bfloat16
I32 = jnp.int32

D_MODEL = 1024
HEAD_DIM = 64
A_HEADS = 16
KV_HEADS = 4
GROUP = 4
KV_WIDTH = KV_HEADS * HEAD_DIM
WINDOW = 128
PAST_LEN = 8192
ROT_DIM = 16
ROPE_THETA = 500000.0
B_KEY = 128
B_VAL = 128
B_HEADS = 8
CHUNK = 64
N_EXPERTS = 32
TOP_K = 4
D_FF = 1024
SWIGLU_ALPHA = 1.702
SWIGLU_LIMIT = 7.0
NORM_EPS = 1e-5
LANES = 128
PAIR_BLK = 2 * LANES

SEG_QA, SEG_QB, SEG_FB, SEG_IB, SEG_OG, SEG_GA, SEG_GB = range(7)
KV_BLK_K = 7 * D_MODEL // KV_WIDTH
KV_BLK_V = KV_BLK_K + 1
IN_DIM = 7 * D_MODEL + 2 * KV_WIDTH

ROW_TILE_IN = 512
COL_TILE_IN = 1536
ROW_TILE_OUT = 256
ROW_TILE_ROUTE = 512
ROW_TILE_MOVE = 128
ROW_TILE_EXPERT = 256
SEQS_PER_STEP = 8

VMEM_LIMIT = 56 * 1024 * 1024


def _cparams(*sem):
    return pltpu.CompilerParams(dimension_semantics=sem, vmem_limit_bytes=VMEM_LIMIT)


def _tile(n, pref):
    t = pref
    while t > LANES and n % t:
        t //= 2
    assert n % t == 0, (n, pref)
    return t


def _sigmoid(x):
    return 1.0 / (1.0 + jnp.exp(-x))


def _silu(x):
    return x * _sigmoid(x)


def _inproj_kernel(x_ref, g_ref, w_ref, z_ref, h_scr):
    @pl.when(pl.program_id(1) == 0)
    def _():
        x = x_ref[...]
        ms = jnp.mean(x * x, axis=-1, keepdims=True)
        h_scr[...] = ((x * lax.rsqrt(ms + NORM_EPS)) * g_ref[...]).astype(BF16)

    z_ref[...] = jnp.dot(h_scr[...], w_ref[...], preferred_element_type=F32)


def _inproj(x, gain, w_bf16):
    t = x.shape[0]
    tm, tn = _tile(t, ROW_TILE_IN), COL_TILE_IN
    return pl.pallas_call(
        _inproj_kernel,
        grid=(t // tm, IN_DIM // tn),
        in_specs=[
            pl.BlockSpec((tm, D_MODEL), lambda i, j: (i, 0)),
            pl.BlockSpec((1, D_MODEL), lambda i, j: (0, 0)),
            pl.BlockSpec((D_MODEL, tn), lambda i, j: (0, j)),
        ],
        out_specs=pl.BlockSpec((tm, tn), lambda i, j: (i, j)),
        out_shape=jax.ShapeDtypeStruct((t, IN_DIM), F32),
        scratch_shapes=[pltpu.VMEM((tm, D_MODEL), BF16)],
        compiler_params=_cparams("arbitrary", "arbitrary"),
        name="inproj",
    )(x, gain.reshape(1, D_MODEL), w_bf16)


def _rope_tables(pos):
    half = ROT_DIM // 2
    inv_freq = ROPE_THETA ** (-2.0 * jnp.arange(half, dtype=F32) / ROT_DIM)
    ang = pos.astype(F32)[:, None] * inv_freq[None, :]
    cos, sin = jnp.cos(ang), jnp.sin(ang)
    n = pos.shape[0]
    rest = HEAD_DIM - ROT_DIM
    c_head = jnp.concatenate([cos, cos, jnp.ones((n, rest), F32)], axis=1)
    a_head = jnp.concatenate([-sin, jnp.zeros((n, half + rest), F32)], axis=1)
    b_head = jnp.concatenate([jnp.zeros((n, half), F32), sin, jnp.zeros((n, rest), F32)], axis=1)
    reps = LANES // HEAD_DIM
    return (jnp.tile(c_head, (1, reps)), jnp.tile(a_head, (1, reps)), jnp.tile(b_head, (1, reps)))


def _rope_slab(x, c, a, b):
    half = ROT_DIM // 2
    up = pltpu.roll(x, LANES - half, 1)
    dn = pltpu.roll(x, half, 1)
    return x * c + up * a + dn * b


def _rope(x, c, a, b):
    slabs = [_rope_slab(x[:, s * LANES:(s + 1) * LANES], c, a, b) for s in range(x.shape[1] // LANES)]
    return jnp.concatenate(slabs, axis=1)


def _attn_prompt_kernel(sink_ref, q_ref, k_ref, v_ref, c_ref, a_ref, b_ref,
                        o_ref, kn_ref, vn_ref, kprev, vprev):
    i = pl.program_id(1)
    w = WINDOW

    @pl.when(i == 0)
    def _():
        kprev[...] = jnp.zeros_like(kprev)
        vprev[...] = jnp.zeros_like(vprev)

    c, a, b = c_ref[...], a_ref[...], b_ref[...]
    q = (_rope(q_ref[...], c, a, b) * (HEAD_DIM ** -0.5)).astype(BF16)
    k_rot = _rope(k_ref[...], c, a, b)
    v_cur = v_ref[...]
    kn_ref[0] = k_rot
    vn_ref[0] = v_cur
    k_cur = k_rot.astype(BF16)
    v_curb = v_cur.astype(BF16)
    gw = GROUP * w
    row = lax.broadcasted_iota(I32, (gw, w), 0)
    r = row & (w - 1)
    col = lax.broadcasted_iota(I32, (gw, w), 1)
    valid_p = (col >= jnp.where(i > 0, r, w))
    valid_c = col <= r
    head_of_row = lax.broadcasted_iota(I32, (gw, 1), 0) // w
    nt = (((1,), (1,)), ((), ()))

    for j in range(KV_HEADS):
        js = slice(j * HEAD_DIM, (j + 1) * HEAD_DIM)
        qj = jnp.concatenate([q[:, (j * GROUP + g) * HEAD_DIM:(j * GROUP + g + 1) * HEAD_DIM]
                              for g in range(GROUP)], axis=0)
        sink = jnp.zeros((gw, 1), F32)
        for g in range(GROUP):
            sink = jnp.where(head_of_row == g, sink_ref[j * GROUP + g], sink)
        s_p = jnp.where(valid_p, lax.dot_general(qj, kprev[j], nt, preferred_element_type=F32), -jnp.inf)
        s_c = jnp.where(valid_c, lax.dot_general(qj, k_cur[:, js], nt, preferred_element_type=F32), -jnp.inf)
        m = jnp.maximum(jnp.maximum(jnp.max(s_p, axis=-1, keepdims=True),
                                    jnp.max(s_c, axis=-1, keepdims=True)), sink)
        p_p = jnp.exp(s_p - m)
        p_c = jnp.exp(s_c - m)
        denom = (jnp.sum(p_p, axis=-1, keepdims=True) + jnp.sum(p_c, axis=-1, keepdims=True)
                 + jnp.exp(sink - m))
        inv = 1.0 / denom
        o = (jnp.dot((p_p * inv).astype(BF16), vprev[j], preferred_element_type=F32)
             + jnp.dot((p_c * inv).astype(BF16), v_curb[:, js], preferred_element_type=F32))
        for g in range(GROUP):
            h = j * GROUP + g
            o_ref[:, h * HEAD_DIM:(h + 1) * HEAD_DIM] = o[g * w:(g + 1) * w, :]

    for j in range(KV_HEADS):
        kprev[j] = k_cur[:, j * HEAD_DIM:(j + 1) * HEAD_DIM]
        vprev[j] = v_curb[:, j * HEAD_DIM:(j + 1) * HEAD_DIM]


def _attn_prompt(z, sinks, tables, bsz, seq):
    nb = seq // WINDOW
    w = WINDOW
    row = lambda b, i: b * nb + i
    return pl.pallas_call(
        _attn_prompt_kernel,
        grid=(bsz, nb),
        in_specs=[
            pl.BlockSpec(memory_space=pltpu.SMEM),
            pl.BlockSpec((w, D_MODEL), lambda b, i: (row(b, i), SEG_QA)),
            pl.BlockSpec((w, KV_WIDTH), lambda b, i: (row(b, i), KV_BLK_K)),
            pl.BlockSpec((w, KV_WIDTH), lambda b, i: (row(b, i), KV_BLK_V)),
            pl.BlockSpec((w, LANES), lambda b, i: (i, 0)),
            pl.BlockSpec((w, LANES), lambda b, i: (i, 0)),
            pl.BlockSpec((w, LANES), lambda b, i: (i, 0)),
        ],
        out_specs=[
            pl.BlockSpec((w, D_MODEL), lambda b, i: (row(b, i), 0)),
            pl.BlockSpec((1, w, KV_WIDTH), lambda b, i: (b, 0, 0)),
            pl.BlockSpec((1, w, KV_WIDTH), lambda b, i: (b, 0, 0)),
        ],
        out_shape=[
            jax.ShapeDtypeStruct((bsz * seq, D_MODEL), F32),
            jax.ShapeDtypeStruct((bsz, w, KV_WIDTH), F32),
            jax.ShapeDtypeStruct((bsz, w, KV_WIDTH), F32),
        ],
        scratch_shapes=[pltpu.VMEM((KV_HEADS, w, HEAD_DIM), BF16), pltpu.VMEM((KV_HEADS, w, HEAD_DIM), BF16)],
        compiler_params=_cparams("arbitrary", "arbitrary"),
        name="attn_prompt",
    )(sinks, z, z, z, *tables)


def _attn_sample_kernel(sink_ref, q_ref, k_ref, v_ref, kc_ref, vc_ref, c_ref, a_ref, b_ref,
                        o_ref, kn_ref, vn_ref, *, n_new):
    nsq = SEQS_PER_STEP
    rows = kc_ref.shape[1]
    c, a, b = c_ref[...], a_ref[...], b_ref[...]
    tq = lax.broadcasted_iota(I32, (nsq, n_new, rows), 1)
    cc = lax.broadcasted_iota(I32, (nsq, n_new, rows), 2)
    valid_c = cc >= tq
    tn = lax.broadcasted_iota(I32, (nsq, n_new, n_new), 1)
    nn = lax.broadcasted_iota(I32, (nsq, n_new, n_new), 2)
    valid_n = nn <= tn
    qs, kns, vns = [], [], []
    for sq in range(nsq):
        qs.append((_rope(q_ref[sq], c, a, b) * (HEAD_DIM ** -0.5)).astype(BF16))
        k_new = _rope(k_ref[sq], c, a, b)
        v_new = v_ref[sq]
        kn_ref[sq, 0:rows - n_new, :] = kc_ref[sq, n_new:rows, :]
        kn_ref[sq, rows - n_new:rows, :] = k_new
        vn_ref[sq, 0:rows - n_new, :] = vc_ref[sq, n_new:rows, :]
        vn_ref[sq, rows - n_new:rows, :] = v_new
        kns.append(k_new.astype(BF16))
        vns.append(v_new.astype(BF16))
    q3, kn3, vn3 = jnp.stack(qs), jnp.stack(kns), jnp.stack(vns)
    kc3, vc3 = kc_ref[...].astype(BF16), vc_ref[...].astype(BF16)
    qk = (((2,), (2,)), ((0,), (0,)))
    pv = (((2,), (1,)), ((0,), (0,)))
    outs = []
    for h in range(A_HEADS):
        j = h // GROUP
        hs = slice(h * HEAD_DIM, (h + 1) * HEAD_DIM)
        js = slice(j * HEAD_DIM, (j + 1) * HEAD_DIM)
        qh = q3[:, :, hs]
        s_c = jnp.where(valid_c, lax.dot_general(qh, kc3[:, :, js], qk, preferred_element_type=F32), -jnp.inf)
        s_n = jnp.where(valid_n, lax.dot_general(qh, kn3[:, :, js], qk, preferred_element_type=F32), -jnp.inf)
        sink = sink_ref[h]
        m = jnp.maximum(jnp.maximum(jnp.max(s_c, axis=-1, keepdims=True),
                                    jnp.max(s_n, axis=-1, keepdims=True)), sink)
        p_c = jnp.exp(s_c - m)
        p_n = jnp.exp(s_n - m)
        denom = (jnp.sum(p_c, axis=-1, keepdims=True) + jnp.sum(p_n, axis=-1, keepdims=True)
                 + jnp.exp(sink - m))
        inv = 1.0 / denom
        outs.append(lax.dot_general((p_c * inv).astype(BF16), vc3[:, :, js], pv, preferred_element_type=F32)
                    + lax.dot_general((p_n * inv).astype(BF16), vn3[:, :, js], pv, preferred_element_type=F32))
    o_ref[...] = jnp.concatenate(outs, axis=2)


def _attn_sample(zs, sinks, tables, cache_k, cache_v):
    n_seq, n_new, _ = zs.shape
    rows = cache_k.shape[1]
    g = SEQS_PER_STEP
    kern = functools.partial(_attn_sample_kernel, n_new=n_new)
    return pl.pallas_call(
        kern,
        grid=(n_seq // g,),
        in_specs=[
            pl.BlockSpec(memory_space=pltpu.SMEM),
            pl.BlockSpec((g, n_new, D_MODEL), lambda i: (i, 0, SEG_QA)),
            pl.BlockSpec((g, n_new, KV_WIDTH), lambda i: (i, 0, KV_BLK_K)),
            pl.BlockSpec((g, n_new, KV_WIDTH), lambda i: (i, 0, KV_BLK_V)),
            pl.BlockSpec((g, rows, KV_WIDTH), lambda i: (i, 0, 0)),
            pl.BlockSpec((g, rows, KV_WIDTH), lambda i: (i, 0, 0)),
            pl.BlockSpec((n_new, LANES), lambda i: (0, 0)),
            pl.BlockSpec((n_new, LANES), lambda i: (0, 0)),
            pl.BlockSpec((n_new, LANES), lambda i: (0, 0)),
        ],
        out_specs=[
            pl.BlockSpec((g, n_new, D_MODEL), lambda i: (i, 0, 0)),
            pl.BlockSpec((g, rows, KV_WIDTH), lambda i: (i, 0, 0)),
            pl.BlockSpec((g, rows, KV_WIDTH), lambda i: (i, 0, 0)),
        ],
        out_shape=[
            jax.ShapeDtypeStruct((n_seq, n_new, D_MODEL), F32),
            jax.ShapeDtypeStruct(cache_k.shape, F32),
            jax.ShapeDtypeStruct(cache_v.shape, F32),
        ],
        compiler_params=_cparams("arbitrary"),
        name="attn_sample",
    )(sinks, zs, zs, zs, cache_k, cache_v, *tables)


def _hgrn_levels(c):
    lv = []
    hb = c // 2
    while hb >= 1:
        lv.append(hb)
        hb //= 2
    return lv


def _hgrn_prompt_kernel(qb_ref, fb_ref, ib_ref, og_ref, lb_ref, hn_ref, tri_ref,
                        o_ref, st_ref, g_scr, state):
    ci = pl.program_id(1)
    nc = pl.num_programs(1)
    c = CHUNK

    @pl.when(ci == 0)
    def _():
        state[...] = jnp.zeros_like(state)

    lb = lb_ref[...]
    f = lb + (1.0 - lb) * _sigmoid(fb_ref[...])
    logf = jnp.log(f)
    g_scr[...] = jnp.dot(tri_ref[...], logf, preferred_element_type=F32, precision=lax.Precision.HIGHEST)

    rt = lax.broadcasted_iota(I32, (c, c), 0)
    cs = lax.broadcasted_iota(I32, (c, c), 1)
    masks = []
    for hb in _hgrn_levels(c):
        same = (rt // (2 * hb)) == (cs // (2 * hb))
        masks.append(same & ((rt // hb) % 2 == 1) & ((cs // hb) % 2 == 0))
    row_t = lax.broadcasted_iota(I32, (8, LANES), 0)

    for h in range(B_HEADS):
        hs = slice(h * B_KEY, (h + 1) * B_KEY)
        qh = _silu(qb_ref[:, hs]) * (B_KEY ** -0.5)
        fh = f[:, hs]
        kk = 1.0 - fh
        v = ib_ref[:, hs]
        vb = v.astype(BF16)
        g = g_scr[:, hs]
        scores = jnp.zeros((c, c), F32)
        def bcast_row(r):
            return jnp.broadcast_to(g_scr[pl.ds(r, 1), hs], (8, B_KEY))

        for lvl, hb in enumerate(_hgrn_levels(c)):
            tiles = []
            for tix in range(c // 8):
                base = tix * 8
                if hb >= 8:
                    tiles.append(bcast_row((base // (2 * hb)) * 2 * hb + hb - 1))
                else:
                    refs = sorted({(tt // (2 * hb)) * 2 * hb + hb - 1 for tt in range(8)})
                    cur = bcast_row(base + refs[0])
                    for rr in refs[1:]:
                        cur = jnp.where(row_t >= (rr - hb + 1), bcast_row(base + rr), cur)
                    tiles.append(cur)
            ref = jnp.concatenate(tiles, axis=0)
            e = jnp.exp(-jnp.abs(g - ref))
            qs = (qh * e).astype(BF16)
            ks = (kk * e).astype(BF16)
            part = lax.dot_general(qs, ks, (((1,), (1,)), ((), ())), preferred_element_type=F32)
            scores = scores + jnp.where(masks[lvl], part, 0.0)
        diag = jnp.sum(qh * kk, axis=-1, keepdims=True)
        o_intra = jnp.dot(scores.astype(BF16), vb, preferred_element_type=F32) + diag * v
        st = state[h]
        qg = (qh * jnp.exp(g)).astype(BF16)
        o_inter = lax.dot_general(qg, st.astype(BF16), (((1,), (1,)), ((), ())), preferred_element_type=F32)
        g_end = g[c - 1:c, :]
        kd = (kk * jnp.exp(g_end - g)).astype(BF16)
        upd = lax.dot_general(vb, kd, (((0,), (0,)), ((), ())), preferred_element_type=F32)
        state[h] = jnp.exp(g_end) * st + upd
        o = o_inter + o_intra
        ms = jnp.mean(o * o, axis=-1, keepdims=True)
        o = (o * lax.rsqrt(ms + NORM_EPS)) * hn_ref[...]
        o_ref[:, hs] = o * _silu(og_ref[:, hs])

    @pl.when(ci == nc - 1)
    def _():
        st_ref[0] = state[...]


def _tri(c):
    return jnp.asarray(np.tril(np.ones((c, c), np.float32)))


def _hgrn_prompt(z, lower_bound, hgrn_norm, bsz, seq):
    c = CHUNK
    nc = seq // c
    row = lambda b, i: b * nc + i
    seg = lambda s: pl.BlockSpec((c, D_MODEL), lambda b, i: (row(b, i), s))
    return pl.pallas_call(
        _hgrn_prompt_kernel,
        grid=(bsz, nc),
        in_specs=[
            seg(SEG_QB), seg(SEG_FB), seg(SEG_IB), seg(SEG_OG),
            pl.BlockSpec((1, D_MODEL), lambda b, i: (0, 0)),
            pl.BlockSpec((1, B_VAL), lambda b, i: (0, 0)),
            pl.BlockSpec((c, c), lambda b, i: (0, 0)),
        ],
        out_specs=[
            pl.BlockSpec((c, D_MODEL), lambda b, i: (row(b, i), 0)),
            pl.BlockSpec((1, B_HEADS, B_VAL, B_KEY), lambda b, i: (b, 0, 0, 0)),
        ],
        out_shape=[
            jax.ShapeDtypeStruct((bsz * seq, D_MODEL), F32),
            jax.ShapeDtypeStruct((bsz, B_HEADS, B_VAL, B_KEY), F32),
        ],
        scratch_shapes=[pltpu.VMEM((c, D_MODEL), F32), pltpu.VMEM((B_HEADS, B_VAL, B_KEY), F32)],
        compiler_params=_cparams("arbitrary", "arbitrary"),
        name="hgrn_prompt",
    )(z, z, z, z, lower_bound.reshape(1, D_MODEL), hgrn_norm.reshape(1, B_VAL), _tri(c))


def _hgrn_sample_kernel(qb_ref, fb_ref, ib_ref, og_ref, s_ref, lb_ref, hn_ref,
                        o_ref, sn_ref, *, n_new):
    lb = lb_ref[...]
    rows_t = lax.broadcasted_iota(I32, (n_new, 1), 0)
    for sq in range(SEQS_PER_STEP):
        f = lb + (1.0 - lb) * _sigmoid(fb_ref[sq])
        logf = jnp.log(f)
        g_rows = [logf[0:1, :]]
        for t in range(1, n_new):
            g_rows.append(g_rows[-1] + logf[t:t + 1, :])
        g_all = jnp.concatenate(g_rows, axis=0)
        qh_all = _silu(qb_ref[sq]) * (B_KEY ** -0.5)
        kk_all = 1.0 - f
        v_all = ib_ref[sq]
        og_all = og_ref[sq]
        outs = []
        for h in range(B_HEADS):
            hs = slice(h * B_KEY, (h + 1) * B_KEY)
            g, qh, kk, v = g_all[:, hs], qh_all[:, hs], kk_all[:, hs], v_all[:, hs]
            s0 = s_ref[sq, h]
            o = jnp.dot((qh * jnp.exp(g)).astype(BF16), s0.astype(BF16), preferred_element_type=F32)
            for s in range(n_new):
                diff = jnp.where(rows_t >= s, g - g[s:s + 1, :], 0.0)
                sc = jnp.sum(qh * kk[s:s + 1, :] * jnp.exp(diff), axis=-1, keepdims=True)
                sc = jnp.where(rows_t >= s, sc, 0.0)
                o = o + sc * v[s:s + 1, :]
            g_end = g[n_new - 1:n_new, :]
            kd = (kk * jnp.exp(g_end - g)).astype(BF16)
            upd = lax.dot_general(kd, v.astype(BF16), (((0,), (0,)), ((), ())), preferred_element_type=F32)
            decay = jnp.transpose(jnp.broadcast_to(jnp.exp(g_end), (B_VAL, B_KEY)))
            sn_ref[sq, h] = decay * s0 + upd
            ms = jnp.mean(o * o, axis=-1, keepdims=True)
            o = (o * lax.rsqrt(ms + NORM_EPS)) * hn_ref[...]
            outs.append(o * _silu(og_all[:, hs]))
        o_ref[sq] = jnp.concatenate(outs, axis=1)


def _hgrn_sample(zs, state, lower_bound, hgrn_norm):
    n_seq, n_new, _ = zs.shape
    g = SEQS_PER_STEP
    seg = lambda s: pl.BlockSpec((g, n_new, D_MODEL), lambda i: (i, 0, s))
    st_spec = pl.BlockSpec((g, B_HEADS, B_KEY, B_VAL), lambda i: (i, 0, 0, 0))
    kern = functools.partial(_hgrn_sample_kernel, n_new=n_new)
    return pl.pallas_call(
        kern,
        grid=(n_seq // g,),
        in_specs=[
            seg(SEG_QB), seg(SEG_FB), seg(SEG_IB), seg(SEG_OG), st_spec,
            pl.BlockSpec((1, D_MODEL), lambda i: (0, 0)),
            pl.BlockSpec((1, B_VAL), lambda i: (0, 0)),
        ],
        out_specs=[pl.BlockSpec((g, n_new, D_MODEL), lambda i: (i, 0, 0)), st_spec],
        out_shape=[jax.ShapeDtypeStruct((n_seq, n_new, D_MODEL), F32), jax.ShapeDtypeStruct(state.shape, F32)],
        compiler_params=_cparams("arbitrary"),
        name="hgrn_sample",
    )(zs, zs, zs, zs, state, lower_bound.reshape(1, D_MODEL), hgrn_norm.reshape(1, B_VAL))


def _merge_kernel(ga_ref, gb_ref, ap_ref, as_ref, bp_ref, bs_ref, x_ref, wo_ref, gn_ref, wr_ref, br_ref,
                  xn_ref, h_ref, lg_ref, *, prompt_tiles):
    is_prompt = pl.program_id(0) < prompt_tiles
    a = jnp.where(is_prompt, ap_ref[...], as_ref[...])
    b = jnp.where(is_prompt, bp_ref[...], bs_ref[...])
    merged = _sigmoid(ga_ref[...]) * a + _sigmoid(gb_ref[...]) * b
    x = x_ref[...] + jnp.dot(merged.astype(BF16), wo_ref[...], preferred_element_type=F32)
    xn_ref[...] = x
    ms = jnp.mean(x * x, axis=-1, keepdims=True)
    h = (x * lax.rsqrt(ms + NORM_EPS)) * gn_ref[...]
    h_ref[...] = h
    lg_ref[...] = jnp.dot(h.astype(BF16), wr_ref[...], preferred_element_type=F32) + br_ref[...]


def _merge(z, a_p, a_s, b_p, b_s, x, w_out_bf16, norm_ffn, w_router_pad, b_router_pad):
    t = x.shape[0]
    tp, ts = a_p.shape[0], a_s.shape[0]
    tm = _tile(np.gcd(tp, ts), ROW_TILE_OUT)
    npt = tp // tm
    rowspec = lambda w, cb: pl.BlockSpec((tm, w), lambda i: (i, cb))
    pspec = pl.BlockSpec((tm, D_MODEL), lambda i: (jnp.minimum(i, npt - 1), 0))
    sspec = pl.BlockSpec((tm, D_MODEL), lambda i: (jnp.maximum(i - npt, 0), 0))
    const = lambda shape: pl.BlockSpec(shape, lambda i: (0, 0))
    return pl.pallas_call(
        functools.partial(_merge_kernel, prompt_tiles=npt),
        grid=(t // tm,),
        in_specs=[
            rowspec(D_MODEL, SEG_GA), rowspec(D_MODEL, SEG_GB),
            pspec, sspec, pspec, sspec, rowspec(D_MODEL, 0),
            const((D_MODEL, D_MODEL)), const((1, D_MODEL)), const((D_MODEL, LANES)), const((1, LANES)),
        ],
        out_specs=[rowspec(D_MODEL, 0), rowspec(D_MODEL, 0), rowspec(LANES, 0)],
        out_shape=[
            jax.ShapeDtypeStruct((t, D_MODEL), F32),
            jax.ShapeDtypeStruct((t, D_MODEL), F32),
            jax.ShapeDtypeStruct((t, LANES), F32),
        ],
        compiler_params=_cparams("arbitrary"),
        name="merge_outproj",
    )(z, z, a_p, a_s, b_p, b_s, x, w_out_bf16, norm_ffn.reshape(1, D_MODEL), w_router_pad, b_router_pad)


def _route_kernel(lg_ref, tri_ref, eidx_ref, gate_ref, rank_ref, cnt_ref, carry):
    @pl.when(pl.program_id(0) == 0)
    def _():
        carry[...] = jnp.zeros_like(carry)

    tm = lg_ref.shape[0]
    lane = lax.broadcasted_iota(I32, (tm, LANES), 1)
    lane_f = lane.astype(F32)
    l = jnp.where(lane < N_EXPERTS, lg_ref[...], -jnp.inf)
    vals, idxs = [], []
    picked = jnp.zeros((tm, LANES), F32)
    for _ in range(TOP_K):
        m = jnp.max(l, axis=-1, keepdims=True)
        idx = jnp.min(jnp.where(l == m, lane_f, float(LANES)), axis=-1, keepdims=True).astype(I32)
        sel = lane == idx
        vals.append(m)
        idxs.append(idx)
        picked = picked + sel.astype(F32)
        l = jnp.where(sel, -jnp.inf, l)
    exps = [jnp.exp(v - vals[0]) for v in vals]
    total = exps[0]
    for e in exps[1:]:
        total = total + e
    inv = 1.0 / total
    before = jnp.dot(tri_ref[...], picked.astype(BF16), preferred_element_type=F32) + carry[...]
    eidx = jnp.zeros((tm, LANES), I32)
    gate = jnp.zeros((tm, LANES), F32)
    rank = jnp.zeros((tm, LANES), I32)
    for r in range(TOP_K):
        rk = jnp.sum(jnp.where(lane == idxs[r], before, 0.0), axis=-1, keepdims=True)
        eidx = jnp.where(lane == r, idxs[r], eidx)
        gate = jnp.where(lane == r, exps[r] * inv, gate)
        rank = jnp.where(lane == r, rk.astype(I32), rank)
    eidx_ref[...] = eidx
    gate_ref[...] = gate
    rank_ref[...] = rank
    carry[...] = carry[...] + jnp.sum(picked, axis=0, keepdims=True)
    cnt_ref[...] = carry[...]


def _route(logits):
    t = logits.shape[0]
    tm = _tile(t, ROW_TILE_ROUTE)
    tri =jnp.asarray(np.tril(np.ones((tm, tm), np.float32), -1)).astype(BF16)
    rows = pl.BlockSpec((tm, LANES), lambda i: (i, 0))
    return pl.pallas_call(
        _route_kernel,
        grid=(t // tm,),
        in_specs=[rows, pl.BlockSpec((tm, tm), lambda i: (0, 0))],
        out_specs=[rows, rows, rows, pl.BlockSpec((1, LANES), lambda i: (0, 0))],
        out_shape=[
            jax.ShapeDtypeStruct((t, LANES), I32),
            jax.ShapeDtypeStruct((t, LANES), F32),
            jax.ShapeDtypeStruct((t, LANES), I32),
            jax.ShapeDtypeStruct((1, LANES), F32),
        ],
        scratch_shapes=[pltpu.VMEM((1, LANES), F32)],
        compiler_params=_cparams("arbitrary"),
        name="route",
    )(logits, tri)


def _scatter_kernel(dest_ref, h_ref, xs_in, xs_hbm, sem):
    del xs_in
    tm = ROW_TILE_MOVE

    def row_copy(r, d):
        return pltpu.make_async_copy(h_ref.at[pl.ds(r, 1)], xs_hbm.at[pl.ds(d, 1)], sem.at[0])

    def issue(r, carry):
        for k in range(TOP_K):
            row_copy(r, dest_ref[r * TOP_K + k]).start()
        return carry

    lax.fori_loop(0, tm, issue, 0)

    def drain(r, carry):
        for _ in range(TOP_K):
            row_copy(0, 0).wait()
        return carry

    lax.fori_loop(0, tm, drain, 0)


def _scatter_rows(h, dest_flat, xs_zero):
    t = h.shape[0]
    tm = ROW_TILE_MOVE
    return pl.pallas_call(
        _scatter_kernel,
        grid=(t // tm,),
        in_specs=[
            pl.BlockSpec((tm * TOP_K,), lambda i: (i,), memory_space=pltpu.SMEM),
            pl.BlockSpec((tm, D_MODEL), lambda i: (i, 0)),
            pl.BlockSpec(memory_space=pl.ANY),
        ],
        out_specs=pl.BlockSpec(memory_space=pl.ANY),
        out_shape=jax.ShapeDtypeStruct(xs_zero.shape, xs_zero.dtype),
        scratch_shapes=[pltpu.SemaphoreType.DMA((1,))],
        input_output_aliases={2: 0},
        compiler_params=_cparams("arbitrary"),
        name="scatter_rows",
    )(dest_flat, h, xs_zero)


def _pair_split_matrix():
    half = PAIR_BLK // 2
    p = np.zeros((PAIR_BLK, PAIR_BLK), np.float32)
    p[2 * np.arange(half), np.arange(half)] = 1.0
    p[2 * np.arange(half) + 1, half + np.arange(half)] = 1.0
    return jnp.asarray(p).astype(BF16)


def _split_pairs(v):
    lead = v.shape[:-1]
    v = v.reshape(lead + (v.shape[-1] // PAIR_BLK, PAIR_BLK // 2, 2))
    return jnp.swapaxes(v, -1, -2).reshape(lead + (-1,))


def _expert_kernel(blk_ref, nv_ref, xs_ref, wu_ref, bu_ref, wd_ref, bd_ref, perm_ref, o_ref, wu_s, wd_s):
    t = pl.program_id(0)
    half = PAIR_BLK // 2

    @pl.when(t >= nv_ref[0])
    def _():
        o_ref[...] = jnp.zeros_like(o_ref)

    @pl.when((t < nv_ref[0]) & ((t == 0) | (blk_ref[t] != blk_ref[jnp.maximum(t - 1, 0)])))
    def _():
        for cb in range(2 * D_FF // PAIR_BLK):
            cs = slice(cb * PAIR_BLK, (cb + 1) * PAIR_BLK)
            wu_s[:, cs] = jnp.dot(wu_ref[0, 0, :, cs].astype(BF16), perm_ref[...],
                                  preferred_element_type=F32).astype(BF16)
        wd_s[...] = wd_ref[0, 0].astype(BF16)

    @pl.when(t < nv_ref[0])
    def _():
        x = xs_ref[...].astype(BF16)
        h = jnp.dot(x, wu_s[...], preferred_element_type=F32) + bu_ref[0]
        acts = []
        for cb in range(2 * D_FF // PAIR_BLK):
            glu = jnp.minimum(h[:, cb * PAIR_BLK:cb * PAIR_BLK + half], SWIGLU_LIMIT)
            lin = jnp.clip(h[:, cb * PAIR_BLK + half:(cb + 1) * PAIR_BLK], -SWIGLU_LIMIT, SWIGLU_LIMIT)
            acts.append(glu * _sigmoid(SWIGLU_ALPHA * glu) * (lin + 1.0))
        act = jnp.concatenate(acts, axis=1)
        o_ref[...] = jnp.dot(act.astype(BF16), wd_s[...], preferred_element_type=F32) + bd_ref[0]


def _experts(xs, blk_expert, n_valid, layer, w_up, b_up_split, w_down, b_down):
    rows = xs.shape[0]
    tm = ROW_TILE_EXPERT
    tile = lambda i, blk, nv: (jnp.minimum(i, nv[0] - 1), 0)
    wsel = lambda i, blk, nv: (blk[i], 0, 0)
    wsel4 = lambda i, blk, nv: (layer, blk[i], 0, 0)
    grid_spec = pltpu.PrefetchScalarGridSpec(
        num_scalar_prefetch=2,
        grid=(rows // tm,),
        in_specs=[
            pl.BlockSpec((tm, D_MODEL), tile),
            pl.BlockSpec((1, 1, D_MODEL, 2 * D_FF), wsel4),
            pl.BlockSpec((1, 1, 2 * D_FF), wsel),
            pl.BlockSpec((1, 1, D_FF, D_MODEL), wsel4),
            pl.BlockSpec((1, 1, D_MODEL), wsel),
            pl.BlockSpec((PAIR_BLK, PAIR_BLK), lambda i, blk, nv: (0, 0)),
        ],
        out_specs=pl.BlockSpec((tm, D_MODEL), lambda i, blk, nv: (i, 0)),
        scratch_shapes=[pltpu.VMEM((D_MODEL, 2 * D_FF), BF16), pltpu.VMEM((D_FF, D_MODEL), BF16)],
    )
    return pl.pallas_call(
        _expert_kernel,
        grid_spec=grid_spec,
        out_shape=jax.ShapeDtypeStruct((rows, D_MODEL), F32),
        compiler_params=_cparams("arbitrary"),
        name="experts",
    )(blk_expert, n_valid, xs, w_up, b_up_split, w_down, b_down, _pair_split_matrix())


def _combine_kernel(dest_ref, dnext_ref, gate_ref, x_ref, gn_ref, ys_hbm, o_ref, buf, sem, *, final_norm):
    i = pl.program_id(0)
    n = pl.num_programs(0)
    tm = ROW_TILE_MOVE

    def row_copy(d, k, r, slot):
        return pltpu.make_async_copy(ys_hbm.at[pl.ds(d, 1)], buf.at[slot, k, pl.ds(r, 1)], sem.at[slot])

    def issue(idx_ref, slot):
        def body(r, carry):
            for k in range(TOP_K):
                row_copy(idx_ref[r * TOP_K + k], k, r, slot).start()
            return carry
        lax.fori_loop(0, tm, body, 0)

    @pl.when(i == 0)
    def _():
        issue(dest_ref, 0)

    @pl.when(i + 1 < n)
    def _():
        issue(dnext_ref, (i + 1) % 2)

    slot = i % 2

    def drain(r, carry):
        for k in range(TOP_K):
            row_copy(0, k, r, slot).wait()
        return carry

    lax.fori_loop(0, tm, drain, 0)

    gate = gate_ref[...]
    y = x_ref[...]
    for k in range(TOP_K):
        y = y + gate[:, k:k + 1] * buf[slot, k]
    if final_norm:
        ms = jnp.mean(y * y, axis=-1, keepdims=True)
        y = (y * lax.rsqrt(ms + NORM_EPS)) * gn_ref[...]
    o_ref[...] = y


def _combine(x, ys, dest_flat, gates, gain, final_norm):
    t = x.shape[0]
    tm = ROW_TILE_MOVE
    nt = t // tm
    kern = functools.partial(_combine_kernel, final_norm=final_norm)
    return pl.pallas_call(
        kern,
        grid=(nt,),
        in_specs=[
            pl.BlockSpec((tm * TOP_K,), lambda i: (i,), memory_space=pltpu.SMEM),
            pl.BlockSpec((tm * TOP_K,), lambda i: (jnp.minimum(i + 1, nt - 1),), memory_space=pltpu.SMEM),
            pl.BlockSpec((tm, LANES), lambda i: (i, 0)),
            pl.BlockSpec((tm, D_MODEL), lambda i: (i, 0)),
            pl.BlockSpec((1, D_MODEL), lambda i: (0, 0)),
            pl.BlockSpec(memory_space=pl.ANY),
        ],
        out_specs=pl.BlockSpec((tm, D_MODEL), lambda i: (i, 0)),
        out_shape=jax.ShapeDtypeStruct((t, D_MODEL), F32),
        scratch_shapes=[pltpu.VMEM((2, TOP_K, tm, D_MODEL), F32), pltpu.SemaphoreType.DMA((2,))],
        compiler_params=_cparams("arbitrary"),
        name="combine",
    )(dest_flat, dest_flat, gates, x, gain.reshape(1, D_MODEL), ys)


def _moe(x, h, logits, layer, w_up, b_up, w_down, b_down, gain, final_norm):
    t = x.shape[0]
    tm = ROW_TILE_EXPERT
    eidx, gates, rank, counts = _route(logits)
    counts = counts[0, :N_EXPERTS].astype(I32)
    padded = (counts + tm - 1) // tm * tm
    pad_end = jnp.cumsum(padded)
    pad_start = pad_end - padded
    dest = (pad_start[eidx[:, :TOP_K]] + rank[:, :TOP_K]).reshape(-1)
    n_tiles = (t * TOP_K + N_EXPERTS * (tm - 1) + tm - 1) // tm
    tile_start = jnp.arange(n_tiles, dtype=I32) * tm
    blk_expert = jnp.minimum(jnp.sum((pad_end[None, :] <= tile_start[:, None]).astype(I32), axis=1), N_EXPERTS - 1)
    n_valid = (pad_end[-1:] // tm).astype(I32)
    xs = _scatter_rows(h, dest, jnp.zeros((n_tiles * tm, D_MODEL), F32))
    ys = _experts(xs, blk_expert, n_valid, layer, w_up, b_up, w_down, b_down)
    return _combine(x, ys, dest, gates, gain, final_norm)


def kernel(x_prompt, x_sample, cache_k, cache_v, state_hgrn, norm_mix, w_in, attn_sinks, lb_logits,
           hgrn_norm, w_out, norm_ffn, w_router, b_router, w_up, b_up, w_down, b_down, norm_final):
    bsz, seq, _ = x_prompt.shape
    n_seq, n_new, _ = x_sample.shape
    depth = w_in.shape[0]
    rows = cache_k.shape[2]
    tp = bsz * seq

    lb_soft = jax.nn.softmax(lb_logits.astype(F32), axis=0)
    lower_bounds = jnp.cumsum(lb_soft, axis=0) - lb_soft[0:1]

    cuts = np.cumsum([0, D_MODEL, KV_WIDTH, KV_WIDTH, D_MODEL, D_MODEL, D_MODEL, D_MODEL, D_MODEL, D_MODEL])
    part = lambda s: w_in[:, :, cuts[s]:cuts[s + 1]]
    w_in_p = jnp.concatenate([part(0), part(3), part(4), part(5), part(6), part(7), part(8), part(1), part(2)],
                             axis=-1).astype(BF16)
    w_out_b = w_out.astype(BF16)
    b_up_p = _split_pairs(b_up).reshape(depth, N_EXPERTS, 1, 2 * D_FF)
    b_down_r = b_down.reshape(depth, N_EXPERTS, 1, D_MODEL)
    w_router_p = jnp.pad(w_router, ((0, 0), (0, 0), (0, LANES - N_EXPERTS))).astype(BF16)
    b_router_p = jnp.pad(b_router, ((0, 0), (0, LANES - N_EXPERTS))).reshape(depth, 1, LANES)

    tab_p = _rope_tables(jnp.arange(seq, dtype=I32))
    tab_s = _rope_tables(PAST_LEN + jnp.arange(n_new, dtype=I32))

    x = jnp.concatenate([x_prompt.reshape(tp, D_MODEL), x_sample.reshape(n_seq * n_new, D_MODEL)], axis=0)
    kp_l, vp_l, sp_l, ks_l, vs_l, ss_l = [], [], [], [], [], []
    for l in range(depth):
        z = _inproj(x, norm_mix[l], w_in_p[l])
        zs = z[tp:].reshape(n_seq, n_new, IN_DIM)
        a_p, kp, vp = _attn_prompt(z, attn_sinks[l], tab_p, bsz, seq)
        a_s, kx, vx = _attn_sample(zs, attn_sinks[l], tab_s,
                                   cache_k[l].reshape(n_seq, rows, KV_WIDTH),
                                   cache_v[l].reshape(n_seq, rows, KV_WIDTH))
        b_p, sp = _hgrn_prompt(z, lower_bounds[l], hgrn_norm[l], bsz, seq)
        b_s, sx = _hgrn_sample(zs, state_hgrn[l].astype(F32), lower_bounds[l], hgrn_norm[l])
        x_mid, h, logits = _merge(z, a_p, a_s.reshape(-1, D_MODEL), b_p, b_s.reshape(-1, D_MODEL), x,
                                  w_out_b[l], norm_ffn[l], w_router_p[l], b_router_p[l])
        last = l == depth - 1
        x = _moe(x_mid, h, logits, l, w_up, b_up_p[l], w_down, b_down_r[l], norm_final, last)
        kp_l.append(kp.reshape(bsz, WINDOW, KV_HEADS, HEAD_DIM))
        vp_l.append(vp.reshape(bsz, WINDOW, KV_HEADS, HEAD_DIM))
        sp_l.append(jnp.swapaxes(sp, -1, -2))
        ks_l.append(kx.reshape(n_seq, rows, KV_HEADS, HEAD_DIM))
        vs_l.append(vx.reshape(n_seq, rows, KV_HEADS, HEAD_DIM))
        ss_l.append(sx)
    y_prompt = x[:tp].reshape(bsz, seq, D_MODEL)
    y_sample = x[tp:].reshape(n_seq, n_new, D_MODEL)
    return (y_prompt, y_sample, jnp.stack(kp_l), jnp.stack(vp_l), jnp.stack(sp_l).astype(x_prompt.dtype),
            jnp.stack(ks_l), jnp.stack(vs_l), jnp.stack(ss_l).astype(state_hgrn.dtype))
```

```python
import functools

import numpy as np
import jax
import jax.numpy as jnp
from jax import lax
from jax.experimental import pallas as pl
from jax.experimental.pallas import tpu as pltpu

F32 = jnp.float32
BF16 = jnp.bfloat16
I32 = jnp.int32

D_MODEL = 1024
HEAD_DIM = 64
A_HEADS = 16
KV_HEADS = 4
GROUP = 4
KV_WIDTH = KV_HEADS * HEAD_DIM
WINDOW = 128
PAST_LEN = 8192
ROT_DIM = 16
ROPE_THETA = 500000.0
B_KEY = 128
B_VAL = 128
B_HEADS = 8
CHUNK = 64
N_EXPERTS = 32
TOP_K = 4
D_FF = 1024
SWIGLU_ALPHA = 1.702
SWIGLU_LIMIT = 7.0
NORM_EPS = 1e-5
LANES = 128
PAIR_BLK = 2 * LANES

SEG_QA, SEG_QB, SEG_FB, SEG_IB, SEG_OG, SEG_GA, SEG_GB = range(7)
KV_BLK_K = 7 * D_MODEL // KV_WIDTH
KV_BLK_V = KV_BLK_K + 1
IN_DIM = 7 * D_MODEL + 2 * KV_WIDTH

ROW_TILE_IN = 512
COL_TILE_IN = 1536
ROW_TILE_OUT = 256
ROW_TILE_ROUTE = 512
ROW_TILE_MOVE = 128
ROW_TILE_EXPERT = 256
SEQS_PER_STEP = 8
HGRN_CHUNKS_PER_STEP = 4

VMEM_LIMIT = 56 * 1024 * 1024


def _cparams(*sem):
    return pltpu.CompilerParams(dimension_semantics=sem, vmem_limit_bytes=VMEM_LIMIT)


def _tile(n, pref):
    t = pref
    while t > LANES and n % t:
        t //= 2
    assert n % t == 0, (n, pref)
    return t


def _sigmoid(x):
    return 1.0 / (1.0 + jnp.exp(-x))


def _silu(x):
    return x * _sigmoid(x)


def _inproj_kernel(x_ref, g_ref, w_ref, z_ref, h_scr):
    @pl.when(pl.program_id(1) == 0)
    def _():
        x = x_ref[...]
        ms = jnp.mean(x * x, axis=-1, keepdims=True)
        h_scr[...] = ((x * lax.rsqrt(ms + NORM_EPS)) * g_ref[...]).astype(BF16)

    z_ref[...] = jnp.dot(h_scr[...], w_ref[...], preferred_element_type=F32)


def _inproj(x, gain, w_bf16):
    t = x.shape[0]
    tm, tn = _tile(t, ROW_TILE_IN), COL_TILE_IN
    return pl.pallas_call(
        _inproj_kernel,
        grid=(t // tm, IN_DIM // tn),
        in_specs=[
            pl.BlockSpec((tm, D_MODEL), lambda i, j: (i, 0)),
            pl.BlockSpec((1, D_MODEL), lambda i, j: (0, 0)),
            pl.BlockSpec((D_MODEL, tn), lambda i, j: (0, j)),
        ],
        out_specs=pl.BlockSpec((tm, tn), lambda i, j: (i, j)),
        out_shape=jax.ShapeDtypeStruct((t, IN_DIM), F32),
        scratch_shapes=[pltpu.VMEM((tm, D_MODEL), BF16)],
        compiler_params=_cparams("arbitrary", "arbitrary"),
        name="inproj",
    )(x, gain.reshape(1, D_MODEL), w_bf16)


def _rope_tables(pos):
    half = ROT_DIM // 2
    inv_freq = ROPE_THETA ** (-2.0 * jnp.arange(half, dtype=F32) / ROT_DIM)
    ang = pos.astype(F32)[:, None] * inv_freq[None, :]
    cos, sin = jnp.cos(ang), jnp.sin(ang)
    n = pos.shape[0]
    rest = HEAD_DIM - ROT_DIM
    c_head = jnp.concatenate([cos, cos, jnp.ones((n, rest), F32)], axis=1)
    a_head = jnp.concatenate([-sin, jnp.zeros((n, half + rest), F32)], axis=1)
    b_head = jnp.concatenate([jnp.zeros((n, half), F32), sin, jnp.zeros((n, rest), F32)], axis=1)
    reps = LANES // HEAD_DIM
    return (jnp.tile(c_head, (1, reps)), jnp.tile(a_head, (1, reps)), jnp.tile(b_head, (1, reps)))


def _rope_slab(x, c, a, b):
    half = ROT_DIM // 2
    up = pltpu.roll(x, LANES - half, 1)
    dn = pltpu.roll(x, half, 1)
    return x * c + up * a + dn * b


def _rope(x, c, a, b):
    slabs = [_rope_slab(x[:, s * LANES:(s + 1) * LANES], c, a, b) for s in range(x.shape[1] // LANES)]
    return jnp.concatenate(slabs, axis=1)


def _attn_prompt_kernel(sink_ref, q_ref, k_ref, v_ref, c_ref, a_ref, b_ref,
                        o_ref, kn_ref, vn_ref, kprev, vprev):
    i = pl.program_id(1)
    w = WINDOW

    @pl.when(i == 0)
    def _():
        kprev[...] = jnp.zeros_like(kprev)
        vprev[...] = jnp.zeros_like(vprev)

    c, a, b = c_ref[...], a_ref[...], b_ref[...]
    q = (_rope(q_ref[...], c, a, b) * (HEAD_DIM ** -0.5)).astype(BF16)
    k_rot = _rope(k_ref[...], c, a, b)
    v_cur = v_ref[...]
    kn_ref[0] = k_rot
    vn_ref[0] = v_cur
    k_cur = k_rot.astype(BF16)
    v_curb = v_cur.astype(BF16)
    gw = GROUP * w
    row = lax.broadcasted_iota(I32, (gw, w), 0)
    r = row & (w - 1)
    col = lax.broadcasted_iota(I32, (gw, w), 1)
    valid_p = (col >= jnp.where(i > 0, r, w))
    valid_c = col <= r
    head_of_row = lax.broadcasted_iota(I32, (gw, 1), 0) // w
    nt = (((1,), (1,)), ((), ()))

    for j in range(KV_HEADS):
        js = slice(j * HEAD_DIM, (j + 1) * HEAD_DIM)
        qj = jnp.concatenate([q[:, (j * GROUP + g) * HEAD_DIM:(j * GROUP + g + 1) * HEAD_DIM]
                              for g in range(GROUP)], axis=0)
        sink = jnp.zeros((gw, 1), F32)
        for g in range(GROUP):
            sink = jnp.where(head_of_row == g, sink_ref[j * GROUP + g], sink)
        s_p = jnp.where(valid_p, lax.dot_general(qj, kprev[j], nt, preferred_element_type=F32), -jnp.inf)
        s_c = jnp.where(valid_c, lax.dot_general(qj, k_cur[:, js], nt, preferred_element_type=F32), -jnp.inf)
        m = jnp.maximum(jnp.maximum(jnp.max(s_p, axis=-1, keepdims=True),
                                    jnp.max(s_c, axis=-1, keepdims=True)), sink)
        p_p = jnp.exp(s_p - m)
        p_c = jnp.exp(s_c - m)
        denom = (jnp.sum(p_p, axis=-1, keepdims=True) + jnp.sum(p_c, axis=-1, keepdims=True)
                 + jnp.exp(sink - m))
        inv = 1.0 / denom
        o = (jnp.dot((p_p * inv).astype(BF16), vprev[j], preferred_element_type=F32)
             + jnp.dot((p_c * inv).astype(BF16), v_curb[:, js], preferred_element_type=F32))
        for g in range(GROUP):
            h = j * GROUP + g
            o_ref[:, h * HEAD_DIM:(h + 1) * HEAD_DIM] = o[g * w:(g + 1) * w, :]

    for j in range(KV_HEADS):
        kprev[j] = k_cur[:, j * HEAD_DIM:(j + 1) * HEAD_DIM]
        vprev[j] = v_curb[:, j * HEAD_DIM:(j + 1) * HEAD_DIM]


def _attn_prompt(z, sinks, tables, bsz, seq):
    nb = seq // WINDOW
    w = WINDOW
    row = lambda b, i: b * nb + i
    return pl.pallas_call(
        _attn_prompt_kernel,
        grid=(bsz, nb),
        in_specs=[
            pl.BlockSpec(memory_space=pltpu.SMEM),
            pl.BlockSpec((w, D_MODEL), lambda b, i: (row(b, i), SEG_QA)),
            pl.BlockSpec((w, KV_WIDTH), lambda b, i: (row(b, i), KV_BLK_K)),
            pl.BlockSpec((w, KV_WIDTH), lambda b, i: (row(b, i), KV_BLK_V)),
            pl.BlockSpec((w, LANES), lambda b, i: (i, 0)),
            pl.BlockSpec((w, LANES), lambda b, i: (i, 0)),
            pl.BlockSpec((w, LANES), lambda b, i: (i, 0)),
        ],
        out_specs=[
            pl.BlockSpec((w, D_MODEL), lambda b, i: (row(b, i), 0)),
            pl.BlockSpec((1, w, KV_WIDTH), lambda b, i: (b, 0, 0)),
            pl.BlockSpec((1, w, KV_WIDTH), lambda b, i: (b, 0, 0)),
        ],
        out_shape=[
            jax.ShapeDtypeStruct((bsz * seq, D_MODEL), F32),
            jax.ShapeDtypeStruct((bsz, w, KV_WIDTH), F32),
            jax.ShapeDtypeStruct((bsz, w, KV_WIDTH), F32),
        ],
        scratch_shapes=[pltpu.VMEM((KV_HEADS, w, HEAD_DIM), BF16), pltpu.VMEM((KV_HEADS, w, HEAD_DIM), BF16)],
        compiler_params=_cparams("arbitrary", "arbitrary"),
        name="attn_prompt",
    )(sinks, z, z, z, *tables)


def _attn_sample_kernel(sink_ref, q_ref, k_ref, v_ref, kc_ref, vc_ref, c_ref, a_ref, b_ref,
                        o_ref, kn_ref, vn_ref, *, n_new):
    nsq = SEQS_PER_STEP
    rows = kc_ref.shape[1]
    c, a, b = c_ref[...], a_ref[...], b_ref[...]
    tq = lax.broadcasted_iota(I32, (nsq, n_new, rows), 1)
    cc = lax.broadcasted_iota(I32, (nsq, n_new, rows), 2)
    valid_c = cc >= tq
    tn = lax.broadcasted_iota(I32, (nsq, n_new, n_new), 1)
    nn = lax.broadcasted_iota(I32, (nsq, n_new, n_new), 2)
    valid_n = nn <= tn
    qs, kns, vns = [], [], []
    for sq in range(nsq):
        qs.append((_rope(q_ref[sq], c, a, b) * (HEAD_DIM ** -0.5)).astype(BF16))
        k_new = _rope(k_ref[sq], c, a, b)
        v_new = v_ref[sq]
        kn_ref[sq, 0:rows - n_new, :] = kc_ref[sq, n_new:rows, :]
        kn_ref[sq, rows - n_new:rows, :] = k_new
        vn_ref[sq, 0:rows - n_new, :] = vc_ref[sq, n_new:rows, :]
        vn_ref[sq, rows - n_new:rows, :] = v_new
        kns.append(k_new.astype(BF16))
        vns.append(v_new.astype(BF16))
    q3, kn3, vn3 = jnp.stack(qs), jnp.stack(kns), jnp.stack(vns)
    kc3, vc3 = kc_ref[...].astype(BF16), vc_ref[...].astype(BF16)
    qk = (((2,), (2,)), ((0,), (0,)))
    pv = (((2,), (1,)), ((0,), (0,)))
    outs = []
    for h in range(A_HEADS):
        j = h // GROUP
        hs = slice(h * HEAD_DIM, (h + 1) * HEAD_DIM)
        js = slice(j * HEAD_DIM, (j + 1) * HEAD_DIM)
        qh = q3[:, :, hs]
        s_c = jnp.where(valid_c, lax.dot_general(qh, kc3[:, :, js], qk, preferred_element_type=F32), -jnp.inf)
        s_n = jnp.where(valid_n, lax.dot_general(qh, kn3[:, :, js], qk, preferred_element_type=F32), -jnp.inf)
        sink = sink_ref[h]
        m = jnp.maximum(jnp.maximum(jnp.max(s_c, axis=-1, keepdims=True),
                                    jnp.max(s_n, axis=-1, keepdims=True)), sink)
        p_c = jnp.exp(s_c - m)
        p_n = jnp.exp(s_n - m)
        denom = (jnp.sum(p_c, axis=-1, keepdims=True) + jnp.sum(p_n, axis=-1, keepdims=True)
                 + jnp.exp(sink - m))
        inv = 1.0 / denom
        outs.append(lax.dot_general((p_c * inv).astype(BF16), vc3[:, :, js], pv, preferred_element_type=F32)
                    + lax.dot_general((p_n * inv).astype(BF16), vn3[:, :, js], pv, preferred_element_type=F32))
    o_ref[...] = jnp.concatenate(outs, axis=2)


def _attn_sample(zs, sinks, tables, cache_k, cache_v):
    n_seq, n_new, _ = zs.shape
    rows = cache_k.shape[1]
    g = SEQS_PER_STEP
    kern = functools.partial(_attn_sample_kernel, n_new=n_new)
    return pl.pallas_call(
        kern,
        grid=(n_seq // g,),
        in_specs=[
            pl.BlockSpec(memory_space=pltpu.SMEM),
            pl.BlockSpec((g, n_new, D_MODEL), lambda i: (i, 0, SEG_QA)),
            pl.BlockSpec((g, n_new, KV_WIDTH), lambda i: (i, 0, KV_BLK_K)),
            pl.BlockSpec((g, n_new, KV_WIDTH), lambda i: (i, 0, KV_BLK_V)),
            pl.BlockSpec((g, rows, KV_WIDTH), lambda i: (i, 0, 0)),
            pl.BlockSpec((g, rows, KV_WIDTH), lambda i: (i, 0, 0)),
            pl.BlockSpec((n_new, LANES), lambda i: (0, 0)),
            pl.BlockSpec((n_new, LANES), lambda i: (0, 0)),
            pl.BlockSpec((n_new, LANES), lambda i: (0, 0)),
        ],
        out_specs=[
            pl.BlockSpec((g, n_new, D_MODEL), lambda i: (i, 0, 0)),
            pl.BlockSpec((g, rows, KV_WIDTH), lambda i: (i, 0, 0)),
            pl.BlockSpec((g, rows, KV_WIDTH), lambda i: (i, 0, 0)),
        ],
        out_shape=[
            jax.ShapeDtypeStruct((n_seq, n_new, D_MODEL), F32),
            jax.ShapeDtypeStruct(cache_k.shape, F32),
            jax.ShapeDtypeStruct(cache_v.shape, F32),
        ],
        compiler_params=_cparams("arbitrary"),
        name="attn_sample",
    )(sinks, zs, zs, zs, cache_k, cache_v, *tables)


def _hgrn_levels(c):
    lv = []
    hb = c // 2
    while hb >= 1:
        lv.append(hb)
        hb //= 2
    return lv


def _hgrn_prompt_kernel(qb_ref, fb_ref, ib_ref, og_ref, lb_ref, hn_ref, tri_ref,
                        o_ref, st_ref, g_scr, state, *, chunks_per_step):
    step = pl.program_id(1)
    c = CHUNK

    @pl.when(step == 0)
    def _():
        state[...] = jnp.zeros_like(state)

    lb = lb_ref[...]
    rt = lax.broadcasted_iota(I32, (c, c), 0)
    cs = lax.broadcasted_iota(I32, (c, c), 1)
    masks = []
    for hb in _hgrn_levels(c):
        same = (rt // (2 * hb)) == (cs // (2 * hb))
        masks.append(same & ((rt // hb) % 2 == 1) & ((cs // hb) % 2 == 0))
    row_t = lax.broadcasted_iota(I32, (8, LANES), 0)

    def one_chunk(ci, carry):
        rows = pl.ds(pl.multiple_of(ci * c, c), c)
        f = lb + (1.0 - lb) * _sigmoid(fb_ref[rows, :])
        logf = jnp.log(f)
        g_scr[...] = jnp.dot(tri_ref[...], logf, preferred_element_type=F32, precision=lax.Precision.HIGHEST)

        for h in range(B_HEADS):
            hs = slice(h * B_KEY, (h + 1) * B_KEY)
            qh = _silu(qb_ref[rows, hs]) * (B_KEY ** -0.5)
            kk = 1.0 - f[:, hs]
            v = ib_ref[rows, hs]
            vb = v.astype(BF16)
            g = g_scr[:, hs]
            scores = jnp.zeros((c, c), F32)

            def bcast_row(r, hs=hs):
                return jnp.broadcast_to(g_scr[pl.ds(r, 1), hs], (8, B_KEY))

            for lvl, hb in enumerate(_hgrn_levels(c)):
                tiles = []
                for tix in range(c // 8):
                    base = tix * 8
                    if hb >= 8:
                        tiles.append(bcast_row((base // (2 * hb)) * 2 * hb + hb - 1))
                    else:
                        refs = sorted({(tt // (2 * hb)) * 2 * hb + hb - 1 for tt in range(8)})
                        cur = bcast_row(base + refs[0])
                        for rr in refs[1:]:
                            cur = jnp.where(row_t >= (rr - hb + 1), bcast_row(base + rr), cur)
                        tiles.append(cur)
                ref = jnp.concatenate(tiles, axis=0)
                e = jnp.exp(-jnp.abs(g - ref))
                qs = (qh * e).astype(BF16)
                ks = (kk * e).astype(BF16)
                part = lax.dot_general(qs, ks, (((1,), (1,)), ((), ())), preferred_element_type=F32)
                scores = scores + jnp.where(masks[lvl], part, 0.0)
            diag = jnp.sum(qh * kk, axis=-1, keepdims=True)
            o_intra = jnp.dot(scores.astype(BF16), vb, preferred_element_type=F32) + diag * v
            st = state[h]
            qg = (qh * jnp.exp(g)).astype(BF16)
            o_inter = lax.dot_general(qg, st.astype(BF16), (((1,), (1,)), ((), ())), preferred_element_type=F32)
            g_end = g[c - 1:c, :]
            kd = (kk * jnp.exp(g_end - g)).astype(BF16)
            upd = lax.dot_general(vb, kd, (((0,), (0,)), ((), ())), preferred_element_type=F32)
            state[h] = jnp.exp(g_end) * st + upd
            o = o_inter + o_intra
            ms = jnp.mean(o * o, axis=-1, keepdims=True)
            o = (o * lax.rsqrt(ms + NORM_EPS)) * hn_ref[...]
            o_ref[rows, hs] = o * _silu(og_ref[rows, hs])
        return carry

    lax.fori_loop(0, chunks_per_step, one_chunk, 0)

    @pl.when(step == pl.num_programs(1) - 1)
    def _():
        st_ref[0] = state[...]


def _tri(c):
    return jnp.asarray(np.tril(np.ones((c, c), np.float32)))


def _hgrn_prompt(z, lower_bound, hgrn_norm, bsz, seq):
    c = CHUNK
    cps = HGRN_CHUNKS_PER_STEP
    rows = c * cps
    nc = seq // rows
    row = lambda b, i: b * nc + i
    seg = lambda s: pl.BlockSpec((rows, D_MODEL), lambda b, i: (row(b, i), s))
    return pl.pallas_call(
        functools.partial(_hgrn_prompt_kernel, chunks_per_step=cps),
        grid=(bsz, nc),
        in_specs=[
            seg(SEG_QB), seg(SEG_FB), seg(SEG_IB), seg(SEG_OG),
            pl.BlockSpec((1, D_MODEL), lambda b, i: (0, 0)),
            pl.BlockSpec((1, B_VAL), lambda b, i: (0, 0)),
            pl.BlockSpec((c, c), lambda b, i: (0, 0)),
        ],
        out_specs=[
            pl.BlockSpec((rows, D_MODEL), lambda b, i: (row(b, i), 0)),
            pl.BlockSpec((1, B_HEADS, B_VAL, B_KEY), lambda b, i: (b, 0, 0, 0)),
        ],
        out_shape=[
            jax.ShapeDtypeStruct((bsz * seq, D_MODEL), F32),
            jax.ShapeDtypeStruct((bsz, B_HEADS, B_VAL, B_KEY), F32),
        ],
        scratch_shapes=[pltpu.VMEM((c, D_MODEL), F32), pltpu.VMEM((B_HEADS, B_VAL, B_KEY), F32)],
        compiler_params=_cparams("arbitrary", "arbitrary"),
        name="hgrn_prompt",
    )(z, z, z, z, lower_bound.reshape(1, D_MODEL), hgrn_norm.reshape(1, B_VAL), _tri(c))


def _hgrn_sample_kernel(qb_ref, fb_ref, ib_ref, og_ref, s_ref, lb_ref, hn_ref,
                        o_ref, sn_ref, *, n_new):
    lb = lb_ref[...]
    rows_t = lax.broadcasted_iota(I32, (n_new, 1), 0)
    for sq in range(SEQS_PER_STEP):
        f = lb + (1.0 - lb) * _sigmoid(fb_ref[sq])
        logf = jnp.log(f)
        g_rows = [logf[0:1, :]]
        for t in range(1, n_new):
            g_rows.append(g_rows[-1] + logf[t:t + 1, :])
        g_all = jnp.concatenate(g_rows, axis=0)
        qh_all = _silu(qb_ref[sq]) * (B_KEY ** -0.5)
        kk_all = 1.0 - f
        v_all = ib_ref[sq]
        og_all = og_ref[sq]
        outs = []
        for h in range(B_HEADS):
            hs = slice(h * B_KEY, (h + 1) * B_KEY)
            g, qh, kk, v = g_all[:, hs], qh_all[:, hs], kk_all[:, hs], v_all[:, hs]
            s0 = s_ref[sq, h]
            o = jnp.dot((qh * jnp.exp(g)).astype(BF16), s0.astype(BF16), preferred_element_type=F32)
            for s in range(n_new):
                diff = jnp.where(rows_t >= s, g - g[s:s + 1, :], 0.0)
                sc = jnp.sum(qh * kk[s:s + 1, :] * jnp.exp(diff), axis=-1, keepdims=True)
                sc = jnp.where(rows_t >= s, sc, 0.0)
                o = o + sc * v[s:s + 1, :]
            g_end = g[n_new - 1:n_new, :]
            kd = (kk * jnp.exp(g_end - g)).astype(BF16)
            upd = lax.dot_general(kd, v.astype(BF16), (((0,), (0,)), ((), ())), preferred_element_type=F32)
            decay = jnp.transpose(jnp.broadcast_to(jnp.exp(g_end), (B_VAL, B_KEY)))
            sn_ref[sq, h] = decay * s0 + upd
            ms = jnp.mean(o * o, axis=-1, keepdims=True)
            o = (o * lax.rsqrt(ms + NORM_EPS)) * hn_ref[...]
            outs.append(o * _silu(og_all[:, hs]))
        o_ref[sq] = jnp.concatenate(outs, axis=1)


def _hgrn_sample(zs, state, lower_bound, hgrn_norm):
    n_seq, n_new, _ = zs.shape
    g = SEQS_PER_STEP
    seg = lambda s: pl.BlockSpec((g, n_new, D_MODEL), lambda i: (i, 0, s))
    st_spec = pl.BlockSpec((g, B_HEADS, B_KEY, B_VAL), lambda i: (i, 0, 0, 0))
    kern = functools.partial(_hgrn_sample_kernel, n_new=n_new)
    return pl.pallas_call(
        kern,
        grid=(n_seq // g,),
        in_specs=[
            seg(SEG_QB), seg(SEG_FB), seg(SEG_IB), seg(SEG_OG), st_spec,
            pl.BlockSpec((1, D_MODEL), lambda i: (0, 0)),
            pl.BlockSpec((1, B_VAL), lambda i: (0, 0)),
        ],
        out_specs=[pl.BlockSpec((g, n_new, D_MODEL), lambda i: (i, 0, 0)), st_spec],
        out_shape=[jax.ShapeDtypeStruct((n_seq, n_new, D_MODEL), F32), jax.ShapeDtypeStruct(state.shape, F32)],
        compiler_params=_cparams("arbitrary"),
        name="hgrn_sample",
    )(zs, zs, zs, zs, state, lower_bound.reshape(1, D_MODEL), hgrn_norm.reshape(1, B_VAL))


def _merge_kernel(ga_ref, gb_ref, ap_ref, as_ref, bp_ref, bs_ref, x_ref, wo_ref, gn_ref, wr_ref, br_ref,
                  xn_ref, h_ref, lg_ref, *, prompt_tiles):
    is_prompt = pl.program_id(0) < prompt_tiles
    a = jnp.where(is_prompt, ap_ref[...], as_ref[...])
    b = jnp.where(is_prompt, bp_ref[...], bs_ref[...])
    merged = _sigmoid(ga_ref[...]) * a + _sigmoid(gb_ref[...]) * b
    x = x_ref[...] + jnp.dot(merged.astype(BF16), wo_ref[...], preferred_element_type=F32)
    xn_ref[...] = x
    ms = jnp.mean(x * x, axis=-1, keepdims=True)
    h = (x * lax.rsqrt(ms + NORM_EPS)) * gn_ref[...]
    h_ref[...] = h
    lg_ref[...] = jnp.dot(h.astype(BF16), wr_ref[...], preferred_element_type=F32) + br_ref[...]


def _merge(z, a_p, a_s, b_p, b_s, x, w_out_bf16, norm_ffn, w_router_pad, b_router_pad):
    t = x.shape[0]
    tp, ts = a_p.shape[0], a_s.shape[0]
    tm = _tile(np.gcd(tp, ts), ROW_TILE_OUT)
    npt = tp // tm
    rowspec = lambda w, cb: pl.BlockSpec((tm, w), lambda i: (i, cb))
    pspec = pl.BlockSpec((tm, D_MODEL), lambda i: (jnp.minimum(i, npt - 1), 0))
    sspec = pl.BlockSpec((tm, D_MODEL), lambda i: (jnp.maximum(i - npt, 0), 0))
    const = lambda shape: pl.BlockSpec(shape, lambda i: (0, 0))
    return pl.pallas_call(
        functools.partial(_merge_kernel, prompt_tiles=npt),
        grid=(t // tm,),
        in_specs=[
            rowspec(D_MODEL, SEG_GA), rowspec(D_MODEL, SEG_GB),
            pspec, sspec, pspec, sspec, rowspec(D_MODEL, 0),
            const((D_MODEL, D_MODEL)), const((1, D_MODEL)), const((D_MODEL, LANES)), const((1, LANES)),
        ],
        out_specs=[rowspec(D_MODEL, 0), rowspec(D_MODEL, 0), rowspec(LANES, 0)],
        out_shape=[
            jax.ShapeDtypeStruct((t, D_MODEL), F32),
            jax.ShapeDtypeStruct((t, D_MODEL), F32),
            jax.ShapeDtypeStruct((t, LANES), F32),
        ],
        compiler_params=_cparams("arbitrary"),
        name="merge_outproj",
    )(z, z, a_p, a_s, b_p, b_s, x, w_out_bf16, norm_ffn.reshape(1, D_MODEL), w_router_pad, b_router_pad)


def _route_kernel(lg_ref, tri_ref, eidx_ref, gate_ref, rank_ref, cnt_ref, carry):
    @pl.when(pl.program_id(0) == 0)
    def _():
        carry[...] = jnp.zeros_like(carry)

    tm = lg_ref.shape[0]
    lane = lax.broadcasted_iota(I32, (tm, LANES), 1)
    lane_f = lane.astype(F32)
    l = jnp.where(lane < N_EXPERTS, lg_ref[...], -jnp.inf)
    vals, idxs = [], []
    picked = jnp.zeros((tm, LANES), F32)
    for _ in range(TOP_K):
        m = jnp.max(l, axis=-1, keepdims=True)
        idx = jnp.min(jnp.where(l == m, lane_f, float(LANES)), axis=-1, keepdims=True).astype(I32)
        sel = lane == idx
        vals.append(m)
        idxs.append(idx)
        picked = picked + sel.astype(F32)
        l = jnp.where(sel, -jnp.inf, l)
    exps = [jnp.exp(v - vals[0]) for v in vals]
    total = exps[0]
    for e in exps[1:]:
        total = total + e
    inv = 1.0 / total
    before = jnp.dot(tri_ref[...], picked.astype(BF16), preferred_element_type=F32) + carry[...]
    eidx = jnp.zeros((tm, LANES), I32)
    gate = jnp.zeros((tm, LANES), F32)
    rank = jnp.zeros((tm, LANES), I32)
    for r in range(TOP_K):
        rk = jnp.sum(jnp.where(lane == idxs[r], before, 0.0), axis=-1, keepdims=True)
        eidx = jnp.where(lane == r, idxs[r], eidx)
        gate = jnp.where(lane == r, exps[r] * inv, gate)
        rank = jnp.where(lane == r, rk.astype(I32), rank)
    eidx_ref[...] = eidx
    gate_ref[...] = gate
    rank_ref[...] = rank
    carry[...] = carry[...] + jnp.sum(picked, axis=0, keepdims=True)
    cnt_ref[...] = carry[...]


def _route(logits):
    t = logits.shape[0]
    tm = _tile(t, ROW_TILE_ROUTE)
    tri =jnp.asarray(np.tril(np.ones((tm, tm), np.float32), -1)).astype(BF16)
    rows = pl.BlockSpec((tm, LANES), lambda i: (i, 0))
    return pl.pallas_call(
        _route_kernel,
        grid=(t // tm,),
        in_specs=[rows, pl.BlockSpec((tm, tm), lambda i: (0, 0))],
        out_specs=[rows, rows, rows, pl.BlockSpec((1, LANES), lambda i: (0, 0))],
        out_shape=[
            jax.ShapeDtypeStruct((t, LANES), I32),
            jax.ShapeDtypeStruct((t, LANES), F32),
            jax.ShapeDtypeStruct((t, LANES), I32),
            jax.ShapeDtypeStruct((1, LANES), F32),
        ],
        scratch_shapes=[pltpu.VMEM((1, LANES), F32)],
        compiler_params=_cparams("arbitrary"),
        name="route",
    )(logits, tri)


def _scatter_kernel(dest_ref, h_ref, xs_in, xs_hbm, sem):
    del xs_in
    tm = ROW_TILE_MOVE

    def row_copy(r, d):
        return pltpu.make_async_copy(h_ref.at[pl.ds(r, 1)], xs_hbm.at[pl.ds(d, 1)], sem.at[0])

    def issue(r, carry):
        for k in range(TOP_K):
            row_copy(r, dest_ref[r * TOP_K + k]).start(priority=k % 2)
        return carry

    lax.fori_loop(0, tm, issue, 0, unroll=4)

    for _ in range(tm * TOP_K):
        row_copy(0, 0).wait()


def _scatter_rows(h, dest_flat, xs_zero):
    t = h.shape[0]
    tm = ROW_TILE_MOVE
    return pl.pallas_call(
        _scatter_kernel,
        grid=(t // tm,),
        in_specs=[
            pl.BlockSpec((tm * TOP_K,), lambda i: (i,), memory_space=pltpu.SMEM),
            pl.BlockSpec((tm, D_MODEL), lambda i: (i, 0)),
            pl.BlockSpec(memory_space=pl.ANY),
        ],
        out_specs=pl.BlockSpec(memory_space=pl.ANY),
        out_shape=jax.ShapeDtypeStruct(xs_zero.shape, xs_zero.dtype),
        scratch_shapes=[pltpu.SemaphoreType.DMA((1,))],
        input_output_aliases={2: 0},
        compiler_params=_cparams("arbitrary"),
        name="scatter_rows",
    )(dest_flat, h, xs_zero)


def _pair_split_matrix():
    half = PAIR_BLK // 2
    p = np.zeros((PAIR_BLK, PAIR_BLK), np.float32)
    p[2 * np.arange(half), np.arange(half)] = 1.0
    p[2 * np.arange(half) + 1, half + np.arange(half)] = 1.0
    return jnp.asarray(p).astype(BF16)


def _split_pairs(v):
    lead = v.shape[:-1]
    v = v.reshape(lead + (v.shape[-1] // PAIR_BLK, PAIR_BLK // 2, 2))
    return jnp.swapaxes(v, -1, -2).reshape(lead + (-1,))


def _expert_kernel(blk_ref, nv_ref, xs_ref, wu_ref, bu_ref, wd_ref, bd_ref, perm_ref, o_ref, wu_s, wd_s):
    t = pl.program_id(0)
    half = PAIR_BLK // 2

    @pl.when(t >= nv_ref[0])
    def _():
        o_ref[...] = jnp.zeros_like(o_ref)

    @pl.when((t < nv_ref[0]) & ((t == 0) | (blk_ref[t] != blk_ref[jnp.maximum(t - 1, 0)])))
    def _():
        for cb in range(2 * D_FF // PAIR_BLK):
            cs = slice(cb * PAIR_BLK, (cb + 1) * PAIR_BLK)
            wu_s[:, cs] = jnp.dot(wu_ref[0, 0, :, cs].astype(BF16), perm_ref[...],
                                  preferred_element_type=F32).astype(BF16)
        wd_s[...] = wd_ref[0, 0].astype(BF16)

    @pl.when(t < nv_ref[0])
    def _():
        x = xs_ref[...].astype(BF16)
        h = jnp.dot(x, wu_s[...], preferred_element_type=F32) + bu_ref[0]
        acts = []
        for cb in range(2 * D_FF // PAIR_BLK):
            glu = jnp.minimum(h[:, cb * PAIR_BLK:cb * PAIR_BLK + half], SWIGLU_LIMIT)
            lin = jnp.clip(h[:, cb * PAIR_BLK + half:(cb + 1) * PAIR_BLK], -SWIGLU_LIMIT, SWIGLU_LIMIT)
            acts.append(glu * _sigmoid(SWIGLU_ALPHA * glu) * (lin + 1.0))
        act = jnp.concatenate(acts, axis=1)
        o_ref[...] = jnp.dot(act.astype(BF16), wd_s[...], preferred_element_type=F32) + bd_ref[0]


def _experts(xs, blk_expert, n_valid, layer, w_up, b_up_split, w_down, b_down):
    rows = xs.shape[0]
    tm = ROW_TILE_EXPERT
    tile = lambda i, blk, nv: (jnp.minimum(i, nv[0] - 1), 0)
    wsel = lambda i, blk, nv: (blk[i], 0, 0)
    wsel4 = lambda i, blk, nv: (layer, blk[i], 0, 0)
    grid_spec = pltpu.PrefetchScalarGridSpec(
        num_scalar_prefetch=2,
        grid=(rows // tm,),
        in_specs=[
            pl.BlockSpec((tm, D_MODEL), tile),
            pl.BlockSpec((1, 1, D_MODEL, 2 * D_FF), wsel4),
            pl.BlockSpec((1, 1, 2 * D_FF), wsel),
            pl.BlockSpec((1, 1, D_FF, D_MODEL), wsel4),
            pl.BlockSpec((1, 1, D_MODEL), wsel),
            pl.BlockSpec((PAIR_BLK, PAIR_BLK), lambda i, blk, nv: (0, 0)),
        ],
        out_specs=pl.BlockSpec((tm, D_MODEL), lambda i, blk, nv: (i, 0)),
        scratch_shapes=[pltpu.VMEM((D_MODEL, 2 * D_FF), BF16), pltpu.VMEM((D_FF, D_MODEL), BF16)],
    )
    return pl.pallas_call(
        _expert_kernel,
        grid_spec=grid_spec,
        out_shape=jax.ShapeDtypeStruct((rows, D_MODEL), F32),
        compiler_params=_cparams("arbitrary"),
        name="experts",
    )(blk_expert, n_valid, xs, w_up, b_up_split, w_down, b_down, _pair_split_matrix())


def _combine_kernel(dest_ref, dnext_ref, gate_ref, x_ref, gn_ref, ys_hbm, o_ref, buf, sem, *, final_norm):
    i = pl.program_id(0)
    n = pl.num_programs(0)
    tm = ROW_TILE_MOVE

    def row_copy(d, k, r, slot):
        return pltpu.make_async_copy(ys_hbm.at[pl.ds(d, 1)], buf.at[slot, k, pl.ds(r, 1)], sem.at[slot])

    def issue(idx_ref, slot):
        def body(r, carry):
            for k in range(TOP_K):
                row_copy(idx_ref[r * TOP_K + k], k, r, slot).start(priority=k % 2)
            return carry
        lax.fori_loop(0, tm, body, 0, unroll=4)

    @pl.when(i == 0)
    def _():
        issue(dest_ref, 0)

    @pl.when(i + 1 < n)
    def _():
        issue(dnext_ref, (i + 1) % 2)

    slot = i % 2

    for _ in range(tm * TOP_K):
        row_copy(0, 0, 0, slot).wait()

    gate = gate_ref[...]
    y = x_ref[...]
    for k in range(TOP_K):
        y = y + gate[:, k:k + 1] * buf[slot, k]
    if final_norm:
        ms = jnp.mean(y * y, axis=-1, keepdims=True)
        y = (y * lax.rsqrt(ms + NORM_EPS)) * gn_ref[...]
    o_ref[...] = y


def _combine(x, ys, dest_flat, gates, gain, final_norm):
    t = x.shape[0]
    tm = ROW_TILE_MOVE
    nt = t // tm
    kern = functools.partial(_combine_kernel, final_norm=final_norm)
    return pl.pallas_call(
        kern,
        grid=(nt,),
        in_specs=[
            pl.BlockSpec((tm * TOP_K,), lambda i: (i,), memory_space=pltpu.SMEM),
            pl.BlockSpec((tm * TOP_K,), lambda i: (jnp.minimum(i + 1, nt - 1),), memory_space=pltpu.SMEM),
            pl.BlockSpec((tm, LANES), lambda i: (i, 0)),
            pl.BlockSpec((tm, D_MODEL), lambda i: (i, 0)),
            pl.BlockSpec((1, D_MODEL), lambda i: (0, 0)),
            pl.BlockSpec(memory_space=pl.ANY),
        ],
        out_specs=pl.BlockSpec((tm, D_MODEL), lambda i: (i, 0)),
        out_shape=jax.ShapeDtypeStruct((t, D_MODEL), F32),
        scratch_shapes=[pltpu.VMEM((2, TOP_K, tm, D_MODEL), F32), pltpu.SemaphoreType.DMA((2,))],
        compiler_params=_cparams("arbitrary"),
        name="combine",
    )(dest_flat, dest_flat, gates, x, gain.reshape(1, D_MODEL), ys)


def _moe(x, h, logits, layer, w_up, b_up, w_down, b_down, gain, final_norm):
    t = x.shape[0]
    tm = ROW_TILE_EXPERT
    eidx, gates, rank, counts = _route(logits)
    counts = counts[0, :N_EXPERTS].astype(I32)
    padded = (counts + tm - 1) // tm * tm
    pad_end = jnp.cumsum(padded)
    pad_start = pad_end - padded
    dest = (pad_start[eidx[:, :TOP_K]] + rank[:, :TOP_K]).reshape(-1)
    n_tiles = (t * TOP_K + N_EXPERTS * (tm - 1) + tm - 1) // tm
    tile_start = jnp.arange(n_tiles, dtype=I32) * tm
    blk_expert = jnp.minimum(jnp.sum((pad_end[None, :] <= tile_start[:, None]).astype(I32), axis=1), N_EXPERTS - 1)
    n_valid = (pad_end[-1:] // tm).astype(I32)
    xs = _scatter_rows(h, dest, jnp.zeros((n_tiles * tm, D_MODEL), F32))
    ys = _experts(xs, blk_expert, n_valid, layer, w_up, b_up, w_down, b_down)
    return _combine(x, ys, dest, gates, gain, final_norm)


def kernel(x_prompt, x_sample, cache_k, cache_v, state_hgrn, norm_mix, w_in, attn_sinks, lb_logits,
           hgrn_norm, w_out, norm_ffn, w_router, b_router, w_up, b_up, w_down, b_down, norm_final):
    bsz, seq, _ = x_prompt.shape
    n_seq, n_new, _ = x_sample.shape
    depth = w_in.shape[0]
    rows = cache_k.shape[2]
    tp = bsz * seq

    lb_soft = jax.nn.softmax(lb_logits.astype(F32), axis=0)
    lower_bounds = jnp.cumsum(lb_soft, axis=0) - lb_soft[0:1]

    cuts = np.cumsum([0, D_MODEL, KV_WIDTH, KV_WIDTH, D_MODEL, D_MODEL, D_MODEL, D_MODEL, D_MODEL, D_MODEL])
    part = lambda s: w_in[:, :, cuts[s]:cuts[s + 1]]
    w_in_p = jnp.concatenate([part(0), part(3), part(4), part(5), part(6), part(7), part(8), part(1), part(2)],
                             axis=-1).astype(BF16)
    w_out_b = w_out.astype(BF16)
    b_up_p = _split_pairs(b_up).reshape(depth, N_EXPERTS, 1, 2 * D_FF)
    b_down_r = b_down.reshape(depth, N_EXPERTS, 1, D_MODEL)
    w_router_p = jnp.pad(w_router, ((0, 0), (0, 0), (0, LANES - N_EXPERTS))).astype(BF16)
    b_router_p = jnp.pad(b_router, ((0, 0), (0, LANES - N_EXPERTS))).reshape(depth, 1, LANES)

    tab_p = _rope_tables(jnp.arange(seq, dtype=I32))
    tab_s = _rope_tables(PAST_LEN + jnp.arange(n_new, dtype=I32))

    x = jnp.concatenate([x_prompt.reshape(tp, D_MODEL), x_sample.reshape(n_seq * n_new, D_MODEL)], axis=0)
    kp_l, vp_l, sp_l, ks_l, vs_l, ss_l = [], [], [], [], [], []
    for l in range(depth):
        z = _inproj(x, norm_mix[l], w_in_p[l])
        zs = z[tp:].reshape(n_seq, n_new, IN_DIM)
        a_p, kp, vp = _attn_prompt(z, attn_sinks[l], tab_p, bsz, seq)
        a_s, kx, vx = _attn_sample(zs, attn_sinks[l], tab_s,
                                   cache_k[l].reshape(n_seq, rows, KV_WIDTH),
                                   cache_v[l].reshape(n_seq, rows, KV_WIDTH))
        b_p, sp = _hgrn_prompt(z, lower_bounds[l], hgrn_norm[l], bsz, seq)
        b_s, sx = _hgrn_sample(zs, state_hgrn[l].astype(F32), lower_bounds[l], hgrn_norm[l])
        x_mid, h, logits = _merge(z, a_p, a_s.reshape(-1, D_MODEL), b_p, b_s.reshape(-1, D_MODEL), x,
                                  w_out_b[l], norm_ffn[l], w_router_p[l], b_router_p[l])
        last = l == depth - 1
        x = _moe(x_mid, h, logits, l, w_up, b_up_p[l], w_down, b_down_r[l], norm_final, last)
        kp_l.append(kp.reshape(bsz, WINDOW, KV_HEADS, HEAD_DIM))
        vp_l.append(vp.reshape(bsz, WINDOW, KV_HEADS, HEAD_DIM))
        sp_l.append(jnp.swapaxes(sp, -1, -2))
        ks_l.append(kx.reshape(n_seq, rows, KV_HEADS, HEAD_DIM))
        vs_l.append(vx.reshape(n_seq, rows, KV_HEADS, HEAD_DIM))
        ss_l.append(sx)
    y_prompt = x[:tp].reshape(bsz, seq, D_MODEL)
    y_sample = x[tp:].reshape(n_seq, n_new, D_MODEL)
    return (y_prompt, y_sample, jnp.stack(kp_l), jnp.stack(vp_l), jnp.stack(sp_l).astype(x_prompt.dtype),
            jnp.stack(ks_l), jnp.stack(vs_l), jnp.stack(ss_l).astype(state_hgrn.dtype))
```

```python
import functools

import numpy as np
import jax
import jax.numpy as jnp
from jax import lax
from jax.experimental import pallas as pl
from jax.experimental.pallas import tpu as pltpu

F32 = jnp.float32
BF16 = jnp.bfloat16
I32 = jnp.int32

D_MODEL = 1024
HEAD_DIM = 64
A_HEADS = 16
KV_HEADS = 4
GROUP = 4
KV_WIDTH = KV_HEADS * HEAD_DIM
WINDOW = 128
PAST_LEN = 8192
ROT_DIM = 16
ROPE_THETA = 500000.0
B_KEY = 128
B_VAL = 128
B_HEADS = 8
CHUNK = 64
N_EXPERTS = 32
TOP_K = 4
D_FF = 1024
SWIGLU_ALPHA = 1.702
SWIGLU_LIMIT = 7.0
NORM_EPS = 1e-5
LANES = 128
PAIR_BLK = 2 * LANES

SEG_QA, SEG_QB, SEG_FB, SEG_IB, SEG_OG, SEG_GA, SEG_GB = range(7)
KV_BLK_K = 7 * D_MODEL // KV_WIDTH
KV_BLK_V = KV_BLK_K + 1
IN_DIM = 7 * D_MODEL + 2 * KV_WIDTH

ROW_TILE_IN = 512
COL_TILE_IN = 2560
ROW_TILE_OUT = 256
ROW_TILE_ROUTE = 512
ROW_TILE_MOVE = 128
ROW_TILE_EXPERT = 256
SEQS_PER_STEP = 8
HGRN_CHUNKS_PER_STEP = 4

VMEM_LIMIT = 56 * 1024 * 1024


def _cparams(*sem):
    return pltpu.CompilerParams(dimension_semantics=sem, vmem_limit_bytes=VMEM_LIMIT)


def _tile(n, pref):
    t = pref
    while t > LANES and n % t:
        t //= 2
    assert n % t == 0, (n, pref)
    return t


def _sigmoid(x):
    return 1.0 / (1.0 + jnp.exp(-x))


def _silu(x):
    return x * _sigmoid(x)


def _inproj_kernel(x_ref, g_ref, w_ref, z_ref, h_scr):
    @pl.when(pl.program_id(1) == 0)
    def _():
        x = x_ref[...]
        ms = jnp.mean(x * x, axis=-1, keepdims=True)
        h_scr[...] = ((x * lax.rsqrt(ms + NORM_EPS)) * g_ref[...]).astype(BF16)

    z_ref[...] = jnp.dot(h_scr[...], w_ref[...], preferred_element_type=F32)


def _inproj(x, gain, w_bf16):
    t = x.shape[0]
    tm, tn = _tile(t, ROW_TILE_IN), COL_TILE_IN
    return pl.pallas_call(
        _inproj_kernel,
        grid=(t // tm, IN_DIM // tn),
        in_specs=[
            pl.BlockSpec((tm, D_MODEL), lambda i, j: (i, 0)),
            pl.BlockSpec((1, D_MODEL), lambda i, j: (0, 0)),
            pl.BlockSpec((D_MODEL, tn), lambda i, j: (0, j)),
        ],
        out_specs=pl.BlockSpec((tm, tn), lambda i, j: (i, j)),
        out_shape=jax.ShapeDtypeStruct((t, IN_DIM), F32),
        scratch_shapes=[pltpu.VMEM((tm, D_MODEL), BF16)],
        compiler_params=_cparams("arbitrary", "arbitrary"),
        name="inproj",
    )(x, gain.reshape(1, D_MODEL), w_bf16)


def _rope_tables(pos):
    half = ROT_DIM // 2
    inv_freq = ROPE_THETA ** (-2.0 * jnp.arange(half, dtype=F32) / ROT_DIM)
    ang = pos.astype(F32)[:, None] * inv_freq[None, :]
    cos, sin = jnp.cos(ang), jnp.sin(ang)
    n = pos.shape[0]
    rest = HEAD_DIM - ROT_DIM
    c_head = jnp.concatenate([cos, cos, jnp.ones((n, rest), F32)], axis=1)
    a_head = jnp.concatenate([-sin, jnp.zeros((n, half + rest), F32)], axis=1)
    b_head = jnp.concatenate([jnp.zeros((n, half), F32), sin, jnp.zeros((n, rest), F32)], axis=1)
    reps = LANES // HEAD_DIM
    return (jnp.tile(c_head, (1, reps)), jnp.tile(a_head, (1, reps)), jnp.tile(b_head, (1, reps)))


def _rope_slab(x, c, a, b):
    half = ROT_DIM // 2
    up = pltpu.roll(x, LANES - half, 1)
    dn = pltpu.roll(x, half, 1)
    return x * c + up * a + dn * b


def _rope(x, c, a, b):
    slabs = [_rope_slab(x[:, s * LANES:(s + 1) * LANES], c, a, b) for s in range(x.shape[1] // LANES)]
    return jnp.concatenate(slabs, axis=1)


def _attn_prompt_kernel(sink_ref, q_ref, k_ref, v_ref, c_ref, a_ref, b_ref,
                        o_ref, kn_ref, vn_ref, kprev, vprev):
    i = pl.program_id(1)
    w = WINDOW

    @pl.when(i == 0)
    def _():
        kprev[...] = jnp.zeros_like(kprev)
        vprev[...] = jnp.zeros_like(vprev)

    c, a, b = c_ref[...], a_ref[...], b_ref[...]
    q = (_rope(q_ref[...], c, a, b) * (HEAD_DIM ** -0.5)).astype(BF16)
    k_rot = _rope(k_ref[...], c, a, b)
    v_cur = v_ref[...]
    kn_ref[0] = k_rot
    vn_ref[0] = v_cur
    k_cur = k_rot.astype(BF16)
    v_curb = v_cur.astype(BF16)
    gw = GROUP * w
    row = lax.broadcasted_iota(I32, (gw, w), 0)
    r = row & (w - 1)
    col = lax.broadcasted_iota(I32, (gw, w), 1)
    valid_p = (col >= jnp.where(i > 0, r, w))
    valid_c = col <= r
    head_of_row = lax.broadcasted_iota(I32, (gw, 1), 0) // w
    nt = (((1,), (1,)), ((), ()))

    for j in range(KV_HEADS):
        js = slice(j * HEAD_DIM, (j + 1) * HEAD_DIM)
        qj = jnp.concatenate([q[:, (j * GROUP + g) * HEAD_DIM:(j * GROUP + g + 1) * HEAD_DIM]
                              for g in range(GROUP)], axis=0)
        sink = jnp.zeros((gw, 1), F32)
        for g in range(GROUP):
            sink = jnp.where(head_of_row == g, sink_ref[j * GROUP + g], sink)
        s_p = jnp.where(valid_p, lax.dot_general(qj, kprev[j], nt, preferred_element_type=F32), -jnp.inf)
        s_c = jnp.where(valid_c, lax.dot_general(qj, k_cur[:, js], nt, preferred_element_type=F32), -jnp.inf)
        m = jnp.maximum(jnp.maximum(jnp.max(s_p, axis=-1, keepdims=True),
                                    jnp.max(s_c, axis=-1, keepdims=True)), sink)
        p_p = jnp.exp(s_p - m)
        p_c = jnp.exp(s_c - m)
        denom = (jnp.sum(p_p, axis=-1, keepdims=True) + jnp.sum(p_c, axis=-1, keepdims=True)
                 + jnp.exp(sink - m))
        inv = 1.0 / denom
        o = (jnp.dot((p_p * inv).astype(BF16), vprev[j], preferred_element_type=F32)
             + jnp.dot((p_c * inv).astype(BF16), v_curb[:, js], preferred_element_type=F32))
        for g in range(GROUP):
            h = j * GROUP + g
            o_ref[:, h * HEAD_DIM:(h + 1) * HEAD_DIM] = o[g * w:(g + 1) * w, :]

    for j in range(KV_HEADS):
        kprev[j] = k_cur[:, j * HEAD_DIM:(j + 1) * HEAD_DIM]
        vprev[j] = v_curb[:, j * HEAD_DIM:(j + 1) * HEAD_DIM]


def _attn_prompt(z, sinks, tables, bsz, seq):
    nb = seq // WINDOW
    w = WINDOW
    row = lambda b, i: b * nb + i
    return pl.pallas_call(
        _attn_prompt_kernel,
        grid=(bsz, nb),
        in_specs=[
            pl.BlockSpec(memory_space=pltpu.SMEM),
            pl.BlockSpec((w, D_MODEL), lambda b, i: (row(b, i), SEG_QA)),
            pl.BlockSpec((w, KV_WIDTH), lambda b, i: (row(b, i), KV_BLK_K)),
            pl.BlockSpec((w, KV_WIDTH), lambda b, i: (row(b, i), KV_BLK_V)),
            pl.BlockSpec((w, LANES), lambda b, i: (i, 0)),
            pl.BlockSpec((w, LANES), lambda b, i: (i, 0)),
            pl.BlockSpec((w, LANES), lambda b, i: (i, 0)),
        ],
        out_specs=[
            pl.BlockSpec((w, D_MODEL), lambda b, i: (row(b, i), 0)),
            pl.BlockSpec((1, w, KV_WIDTH), lambda b, i: (b, 0, 0)),
            pl.BlockSpec((1, w, KV_WIDTH), lambda b, i: (b, 0, 0)),
        ],
        out_shape=[
            jax.ShapeDtypeStruct((bsz * seq, D_MODEL), F32),
            jax.ShapeDtypeStruct((bsz, w, KV_WIDTH), F32),
            jax.ShapeDtypeStruct((bsz, w, KV_WIDTH), F32),
        ],
        scratch_shapes=[pltpu.VMEM((KV_HEADS, w, HEAD_DIM), BF16), pltpu.VMEM((KV_HEADS, w, HEAD_DIM), BF16)],
        compiler_params=_cparams("arbitrary", "arbitrary"),
        name="attn_prompt",
    )(sinks, z, z, z, *tables)


def _attn_sample_kernel(sink_ref, q_ref, k_ref, v_ref, kc_ref, vc_ref, c_ref, a_ref, b_ref,
                        o_ref, kn_ref, vn_ref, *, n_new):
    nsq = SEQS_PER_STEP
    rows = kc_ref.shape[1]
    c, a, b = c_ref[...], a_ref[...], b_ref[...]
    tq = lax.broadcasted_iota(I32, (nsq, n_new, rows), 1)
    cc = lax.broadcasted_iota(I32, (nsq, n_new, rows), 2)
    valid_c = cc >= tq
    tn = lax.broadcasted_iota(I32, (nsq, n_new, n_new), 1)
    nn = lax.broadcasted_iota(I32, (nsq, n_new, n_new), 2)
    valid_n = nn <= tn
    qs, kns, vns = [], [], []
    for sq in range(nsq):
        qs.append((_rope(q_ref[sq], c, a, b) * (HEAD_DIM ** -0.5)).astype(BF16))
        k_new = _rope(k_ref[sq], c, a, b)
        v_new = v_ref[sq]
        kn_ref[sq, 0:rows - n_new, :] = kc_ref[sq, n_new:rows, :]
        kn_ref[sq, rows - n_new:rows, :] = k_new
        vn_ref[sq, 0:rows - n_new, :] = vc_ref[sq, n_new:rows, :]
        vn_ref[sq, rows - n_new:rows, :] = v_new
        kns.append(k_new.astype(BF16))
        vns.append(v_new.astype(BF16))
    q3, kn3, vn3 = jnp.stack(qs), jnp.stack(kns), jnp.stack(vns)
    kc3, vc3 = kc_ref[...].astype(BF16), vc_ref[...].astype(BF16)
    qk = (((2,), (2,)), ((0,), (0,)))
    pv = (((2,), (1,)), ((0,), (0,)))
    outs = []
    for h in range(A_HEADS):
        j = h // GROUP
        hs = slice(h * HEAD_DIM, (h + 1) * HEAD_DIM)
        js = slice(j * HEAD_DIM, (j + 1) * HEAD_DIM)
        qh = q3[:, :, hs]
        s_c = jnp.where(valid_c, lax.dot_general(qh, kc3[:, :, js], qk, preferred_element_type=F32), -jnp.inf)
        s_n = jnp.where(valid_n, lax.dot_general(qh, kn3[:, :, js], qk, preferred_element_type=F32), -jnp.inf)
        sink = sink_ref[h]
        m = jnp.maximum(jnp.maximum(jnp.max(s_c, axis=-1, keepdims=True),
                                    jnp.max(s_n, axis=-1, keepdims=True)), sink)
        p_c = jnp.exp(s_c - m)
        p_n = jnp.exp(s_n - m)
        denom = (jnp.sum(p_c, axis=-1, keepdims=True) + jnp.sum(p_n, axis=-1, keepdims=True)
                 + jnp.exp(sink - m))
        inv = 1.0 / denom
        outs.append(lax.dot_general((p_c * inv).astype(BF16), vc3[:, :, js], pv, preferred_element_type=F32)
                    + lax.dot_general((p_n * inv).astype(BF16), vn3[:, :, js], pv, preferred_element_type=F32))
    o_ref[...] = jnp.concatenate(outs, axis=2)


def _attn_sample(zs, sinks, tables, cache_k, cache_v):
    n_seq, n_new, _ = zs.shape
    rows = cache_k.shape[1]
    g = SEQS_PER_STEP
    kern = functools.partial(_attn_sample_kernel, n_new=n_new)
    return pl.pallas_call(
        kern,
        grid=(n_seq // g,),
        in_specs=[
            pl.BlockSpec(memory_space=pltpu.SMEM),
            pl.BlockSpec((g, n_new, D_MODEL), lambda i: (i, 0, SEG_QA)),
            pl.BlockSpec((g, n_new, KV_WIDTH), lambda i: (i, 0, KV_BLK_K)),
            pl.BlockSpec((g, n_new, KV_WIDTH), lambda i: (i, 0, KV_BLK_V)),
            pl.BlockSpec((g, rows, KV_WIDTH), lambda i: (i, 0, 0)),
            pl.BlockSpec((g, rows, KV_WIDTH), lambda i: (i, 0, 0)),
            pl.BlockSpec((n_new, LANES), lambda i: (0, 0)),
            pl.BlockSpec((n_new, LANES), lambda i: (0, 0)),
            pl.BlockSpec((n_new, LANES), lambda i: (0, 0)),
        ],
        out_specs=[
            pl.BlockSpec((g, n_new, D_MODEL), lambda i: (i, 0, 0)),
            pl.BlockSpec((g, rows, KV_WIDTH), lambda i: (i, 0, 0)),
            pl.BlockSpec((g, rows, KV_WIDTH), lambda i: (i, 0, 0)),
        ],
        out_shape=[
            jax.ShapeDtypeStruct((n_seq, n_new, D_MODEL), F32),
            jax.ShapeDtypeStruct(cache_k.shape, F32),
            jax.ShapeDtypeStruct(cache_v.shape, F32),
        ],
        compiler_params=_cparams("arbitrary"),
        name="attn_sample",
    )(sinks, zs, zs, zs, cache_k, cache_v, *tables)


def _hgrn_levels(c):
    lv = []
    hb = c // 2
    while hb >= 1:
        lv.append(hb)
        hb //= 2
    return lv


def _hgrn_prompt_kernel(qb_ref, fb_ref, ib_ref, og_ref, lb_ref, hn_ref, tri_ref,
                        o_ref, st_ref, g_scr, state, *, chunks_per_step):
    step = pl.program_id(1)
    c = CHUNK

    @pl.when(step == 0)
    def _():
        state[...] = jnp.zeros_like(state)

    lb = lb_ref[...]
    rt = lax.broadcasted_iota(I32, (c, c), 0)
    cs = lax.broadcasted_iota(I32, (c, c), 1)
    masks = []
    for hb in _hgrn_levels(c):
        same = (rt // (2 * hb)) == (cs // (2 * hb))
        masks.append(same & ((rt // hb) % 2 == 1) & ((cs // hb) % 2 == 0))
    row_t = lax.broadcasted_iota(I32, (8, LANES), 0)

    def one_chunk(ci, carry):
        rows = pl.ds(pl.multiple_of(ci * c, c), c)
        f = lb + (1.0 - lb) * _sigmoid(fb_ref[rows, :])
        logf = jnp.log(f)
        g_scr[...] = jnp.dot(tri_ref[...], logf, preferred_element_type=F32, precision=lax.Precision.HIGHEST)

        for h in range(B_HEADS):
            hs = slice(h * B_KEY, (h + 1) * B_KEY)
            qh = _silu(qb_ref[rows, hs]) * (B_KEY ** -0.5)
            kk = 1.0 - f[:, hs]
            v = ib_ref[rows, hs]
            vb = v.astype(BF16)
            g = g_scr[:, hs]
            scores = jnp.zeros((c, c), F32)

            def bcast_row(r, hs=hs):
                return jnp.broadcast_to(g_scr[pl.ds(r, 1), hs], (8, B_KEY))

            for lvl, hb in enumerate(_hgrn_levels(c)):
                tiles = []
                for tix in range(c // 8):
                    base = tix * 8
                    if hb >= 8:
                        tiles.append(bcast_row((base // (2 * hb)) * 2 * hb + hb - 1))
                    else:
                        refs = sorted({(tt // (2 * hb)) * 2 * hb + hb - 1 for tt in range(8)})
                        cur = bcast_row(base + refs[0])
                        for rr in refs[1:]:
                            cur = jnp.where(row_t >= (rr - hb + 1), bcast_row(base + rr), cur)
                        tiles.append(cur)
                ref = jnp.concatenate(tiles, axis=0)
                e = jnp.exp(-jnp.abs(g - ref))
                qs = (qh * e).astype(BF16)
                ks = (kk * e).astype(BF16)
                part = lax.dot_general(qs, ks, (((1,), (1,)), ((), ())), preferred_element_type=F32)
                scores = scores + jnp.where(masks[lvl], part, 0.0)
            diag = jnp.sum(qh * kk, axis=-1, keepdims=True)
            o_intra = jnp.dot(scores.astype(BF16), vb, preferred_element_type=F32) + diag * v
            st = state[h]
            qg = (qh * jnp.exp(g)).astype(BF16)
            o_inter = lax.dot_general(qg, st.astype(BF16), (((1,), (1,)), ((), ())), preferred_element_type=F32)
            g_end = g[c - 1:c, :]
            kd = (kk * jnp.exp(g_end - g)).astype(BF16)
            upd = lax.dot_general(vb, kd, (((0,), (0,)), ((), ())), preferred_element_type=F32)
            state[h] = jnp.exp(g_end) * st + upd
            o = o_inter + o_intra
            ms = jnp.mean(o * o, axis=-1, keepdims=True)
            o = (o * lax.rsqrt(ms + NORM_EPS)) * hn_ref[...]
            o_ref[rows, hs] = o * _silu(og_ref[rows, hs])
        return carry

    lax.fori_loop(0, chunks_per_step, one_chunk, 0)

    @pl.when(step == pl.num_programs(1) - 1)
    def _():
        st_ref[0] = state[...]


def _tri(c):
    return jnp.asarray(np.tril(np.ones((c, c), np.float32)))


def _hgrn_prompt(z, lower_bound, hgrn_norm, bsz, seq):
    c = CHUNK
    cps = HGRN_CHUNKS_PER_STEP
    rows = c * cps
    nc = seq // rows
    row = lambda b, i: b * nc + i
    seg = lambda s: pl.BlockSpec((rows, D_MODEL), lambda b, i: (row(b, i), s))
    return pl.pallas_call(
        functools.partial(_hgrn_prompt_kernel, chunks_per_step=cps),
        grid=(bsz, nc),
        in_specs=[
            seg(SEG_QB), seg(SEG_FB), seg(SEG_IB), seg(SEG_OG),
            pl.BlockSpec((1, D_MODEL), lambda b, i: (0, 0)),
            pl.BlockSpec((1, B_VAL), lambda b, i: (0, 0)),
            pl.BlockSpec((c, c), lambda b, i: (0, 0)),
        ],
        out_specs=[
            pl.BlockSpec((rows, D_MODEL), lambda b, i: (row(b, i), 0)),
            pl.BlockSpec((1, B_HEADS, B_VAL, B_KEY), lambda b, i: (b, 0, 0, 0)),
        ],
        out_shape=[
            jax.ShapeDtypeStruct((bsz * seq, D_MODEL), F32),
            jax.ShapeDtypeStruct((bsz, B_HEADS, B_VAL, B_KEY), F32),
        ],
        scratch_shapes=[pltpu.VMEM((c, D_MODEL), F32), pltpu.VMEM((B_HEADS, B_VAL, B_KEY), F32)],
        compiler_params=_cparams("arbitrary", "arbitrary"),
        name="hgrn_prompt",
    )(z, z, z, z, lower_bound.reshape(1, D_MODEL), hgrn_norm.reshape(1, B_VAL), _tri(c))


def _hgrn_sample_kernel(qb_ref, fb_ref, ib_ref, og_ref, s_ref, lb_ref, hn_ref,
                        o_ref, sn_ref, *, n_new):
    lb = lb_ref[...]
    rows_t = lax.broadcasted_iota(I32, (n_new, 1), 0)
    for sq in range(SEQS_PER_STEP):
        f = lb + (1.0 - lb) * _sigmoid(fb_ref[sq])
        logf = jnp.log(f)
        g_rows = [logf[0:1, :]]
        for t in range(1, n_new):
            g_rows.append(g_rows[-1] + logf[t:t + 1, :])
        g_all = jnp.concatenate(g_rows, axis=0)
        qh_all = _silu(qb_ref[sq]) * (B_KEY ** -0.5)
        kk_all = 1.0 - f
        v_all = ib_ref[sq]
        og_all = og_ref[sq]
        outs = []
        for h in range(B_HEADS):
            hs = slice(h * B_KEY, (h + 1) * B_KEY)
            g, qh, kk, v = g_all[:, hs], qh_all[:, hs], kk_all[:, hs], v_all[:, hs]
            s0 = s_ref[sq, h]
            o = jnp.dot((qh * jnp.exp(g)).astype(BF16), s0.astype(BF16), preferred_element_type=F32)
            for s in range(n_new):
                diff = jnp.where(rows_t >= s, g - g[s:s + 1, :], 0.0)
                sc = jnp.sum(qh * kk[s:s + 1, :] * jnp.exp(diff), axis=-1, keepdims=True)
                sc = jnp.where(rows_t >= s, sc, 0.0)
                o = o + sc * v[s:s + 1, :]
            g_end = g[n_new - 1:n_new, :]
            kd = (kk * jnp.exp(g_end - g)).astype(BF16)
            upd = lax.dot_general(kd, v.astype(BF16), (((0,), (0,)), ((), ())), preferred_element_type=F32)
            decay = jnp.transpose(jnp.broadcast_to(jnp.exp(g_end), (B_VAL, B_KEY)))
            sn_ref[sq, h] = decay * s0 + upd
            ms = jnp.mean(o * o, axis=-1, keepdims=True)
            o = (o * lax.rsqrt(ms + NORM_EPS)) * hn_ref[...]
            outs.append(o * _silu(og_all[:, hs]))
        o_ref[sq] = jnp.concatenate(outs, axis=1)


def _hgrn_sample(zs, state, lower_bound, hgrn_norm):
    n_seq, n_new, _ = zs.shape
    g = SEQS_PER_STEP
    seg = lambda s: pl.BlockSpec((g, n_new, D_MODEL), lambda i: (i, 0, s))
    st_spec = pl.BlockSpec((g, B_HEADS, B_KEY, B_VAL), lambda i: (i, 0, 0, 0))
    kern = functools.partial(_hgrn_sample_kernel, n_new=n_new)
    return pl.pallas_call(
        kern,
        grid=(n_seq // g,),
        in_specs=[
            seg(SEG_QB), seg(SEG_FB), seg(SEG_IB), seg(SEG_OG), st_spec,
            pl.BlockSpec((1, D_MODEL), lambda i: (0, 0)),
            pl.BlockSpec((1, B_VAL), lambda i: (0, 0)),
        ],
        out_specs=[pl.BlockSpec((g, n_new, D_MODEL), lambda i: (i, 0, 0)), st_spec],
        out_shape=[jax.ShapeDtypeStruct((n_seq, n_new, D_MODEL), F32), jax.ShapeDtypeStruct(state.shape, F32)],
        compiler_params=_cparams("arbitrary"),
        name="hgrn_sample",
    )(zs, zs, zs, zs, state, lower_bound.reshape(1, D_MODEL), hgrn_norm.reshape(1, B_VAL))


def _merge_kernel(ga_ref, gb_ref, ap_ref, as_ref, bp_ref, bs_ref, x_ref, wo_ref, gn_ref, wr_ref, br_ref,
                  xn_ref, h_ref, lg_ref, *, prompt_tiles):
    is_prompt = pl.program_id(0) < prompt_tiles
    a = jnp.where(is_prompt, ap_ref[...], as_ref[...])
    b = jnp.where(is_prompt, bp_ref[...], bs_ref[...])
    merged = _sigmoid(ga_ref[...]) * a + _sigmoid(gb_ref[...]) * b
    x = x_ref[...] + jnp.dot(merged.astype(BF16), wo_ref[...], preferred_element_type=F32)
    xn_ref[...] = x
    ms = jnp.mean(x * x, axis=-1, keepdims=True)
    h = (x * lax.rsqrt(ms + NORM_EPS)) * gn_ref[...]
    h_ref[...] = h
    lg_ref[...] = jnp.dot(h.astype(BF16), wr_ref[...], preferred_element_type=F32) + br_ref[...]


def _merge(z, a_p, a_s, b_p, b_s, x, w_out_bf16, norm_ffn, w_router_pad, b_router_pad):
    t = x.shape[0]
    tp, ts = a_p.shape[0], a_s.shape[0]
    tm = _tile(np.gcd(tp, ts), ROW_TILE_OUT)
    npt = tp // tm
    rowspec = lambda w, cb: pl.BlockSpec((tm, w), lambda i: (i, cb))
    pspec = pl.BlockSpec((tm, D_MODEL), lambda i: (jnp.minimum(i, npt - 1), 0))
    sspec = pl.BlockSpec((tm, D_MODEL), lambda i: (jnp.maximum(i - npt, 0), 0))
    const = lambda shape: pl.BlockSpec(shape, lambda i: (0, 0))
    return pl.pallas_call(
        functools.partial(_merge_kernel, prompt_tiles=npt),
        grid=(t // tm,),
        in_specs=[
            rowspec(D_MODEL, SEG_GA), rowspec(D_MODEL, SEG_GB),
            pspec, sspec, pspec, sspec, rowspec(D_MODEL, 0),
            const((D_MODEL, D_MODEL)), const((1, D_MODEL)), const((D_MODEL, LANES)), const((1, LANES)),
        ],
        out_specs=[rowspec(D_MODEL, 0), rowspec(D_MODEL, 0), rowspec(LANES, 0)],
        out_shape=[
            jax.ShapeDtypeStruct((t, D_MODEL), F32),
            jax.ShapeDtypeStruct((t, D_MODEL), F32),
            jax.ShapeDtypeStruct((t, LANES), F32),
        ],
        compiler_params=_cparams("arbitrary"),
        name="merge_outproj",
    )(z, z, a_p, a_s, b_p, b_s, x, w_out_bf16, norm_ffn.reshape(1, D_MODEL), w_router_pad, b_router_pad)


def _route_kernel(lg_ref, tri_ref, eidx_ref, gate_ref, rank_ref, cnt_ref, carry):
    @pl.when(pl.program_id(0) == 0)
    def _():
        carry[...] = jnp.zeros_like(carry)

    tm = lg_ref.shape[0]
    lane = lax.broadcasted_iota(I32, (tm, LANES), 1)
    lane_f = lane.astype(F32)
    l = jnp.where(lane < N_EXPERTS, lg_ref[...], -jnp.inf)
    vals, idxs = [], []
    picked = jnp.zeros((tm, LANES), F32)
    for _ in range(TOP_K):
        m = jnp.max(l, axis=-1, keepdims=True)
        idx = jnp.min(jnp.where(l == m, lane_f, float(LANES)), axis=-1, keepdims=True).astype(I32)
        sel = lane == idx
        vals.append(m)
        idxs.append(idx)
        picked = picked + sel.astype(F32)
        l = jnp.where(sel, -jnp.inf, l)
    exps = [jnp.exp(v - vals[0]) for v in vals]
    total = exps[0]
    for e in exps[1:]:
        total = total + e
    inv = 1.0 / total
    before = jnp.dot(tri_ref[...], picked.astype(BF16), preferred_element_type=F32) + carry[...]
    eidx = jnp.zeros((tm, LANES), I32)
    gate = jnp.zeros((tm, LANES), F32)
    rank = jnp.zeros((tm, LANES), I32)
    for r in range(TOP_K):
        rk = jnp.sum(jnp.where(lane == idxs[r], before, 0.0), axis=-1, keepdims=True)
        eidx = jnp.where(lane == r, idxs[r], eidx)
        gate = jnp.where(lane == r, exps[r] * inv, gate)
        rank = jnp.where(lane == r, rk.astype(I32), rank)
    eidx_ref[...] = eidx
    gate_ref[...] = gate
    rank_ref[...] = rank
    carry[...] = carry[...] + jnp.sum(picked, axis=0, keepdims=True)
    cnt_ref[...] = carry[...]


def _route(logits):
    t = logits.shape[0]
    tm = _tile(t, ROW_TILE_ROUTE)
    tri =jnp.asarray(np.tril(np.ones((tm, tm), np.float32), -1)).astype(BF16)
    rows = pl.BlockSpec((tm, LANES), lambda i: (i, 0))
    return pl.pallas_call(
        _route_kernel,
        grid=(t // tm,),
        in_specs=[rows, pl.BlockSpec((tm, tm), lambda i: (0, 0))],
        out_specs=[rows, rows, rows, pl.BlockSpec((1, LANES), lambda i: (0, 0))],
        out_shape=[
            jax.ShapeDtypeStruct((t, LANES), I32),
            jax.ShapeDtypeStruct((t, LANES), F32),
            jax.ShapeDtypeStruct((t, LANES), I32),
            jax.ShapeDtypeStruct((1, LANES), F32),
        ],
        scratch_shapes=[pltpu.VMEM((1, LANES), F32)],
        compiler_params=_cparams("arbitrary"),
        name="route",
    )(logits, tri)


def _scatter_kernel(dest_ref, h_ref, xs_in, xs_hbm, sem):
    del xs_in
    tm = ROW_TILE_MOVE

    def row_copy(r, d):
        return pltpu.make_async_copy(h_ref.at[pl.ds(r, 1)], xs_hbm.at[pl.ds(d, 1)], sem.at[0])

    def issue(r, carry):
        for k in range(TOP_K):
            row_copy(r, dest_ref[r * TOP_K + k]).start(priority=k % 2)
        return carry

    lax.fori_loop(0, tm, issue, 0, unroll=4)

    for _ in range(tm * TOP_K):
        row_copy(0, 0).wait()


def _scatter_rows(h, dest_flat, xs_zero):
    t = h.shape[0]
    tm = ROW_TILE_MOVE
    return pl.pallas_call(
        _scatter_kernel,
        grid=(t // tm,),
        in_specs=[
            pl.BlockSpec((tm * TOP_K,), lambda i: (i,), memory_space=pltpu.SMEM),
            pl.BlockSpec((tm, D_MODEL), lambda i: (i, 0)),
            pl.BlockSpec(memory_space=pl.ANY),
        ],
        out_specs=pl.BlockSpec(memory_space=pl.ANY),
        out_shape=jax.ShapeDtypeStruct(xs_zero.shape, xs_zero.dtype),
        scratch_shapes=[pltpu.SemaphoreType.DMA((1,))],
        input_output_aliases={2: 0},
        compiler_params=_cparams("arbitrary"),
        name="scatter_rows",
    )(dest_flat, h, xs_zero)


def _pair_split_matrix():
    half = PAIR_BLK // 2
    p = np.zeros((PAIR_BLK, PAIR_BLK), np.float32)
    p[2 * np.arange(half), np.arange(half)] = 1.0
    p[2 * np.arange(half) + 1, half + np.arange(half)] = 1.0
    return jnp.asarray(p).astype(BF16)


def _split_pairs(v):
    lead = v.shape[:-1]
    v = v.reshape(lead + (v.shape[-1] // PAIR_BLK, PAIR_BLK // 2, 2))
    return jnp.swapaxes(v, -1, -2).reshape(lead + (-1,))


def _expert_kernel(blk_ref, nv_ref, xs_ref, wu_ref, bu_ref, wd_ref, bd_ref, perm_ref, o_ref, wu_s, wd_s):
    t = pl.program_id(0)
    half = PAIR_BLK // 2

    @pl.when(t >= nv_ref[0])
    def _():
        o_ref[...] = jnp.zeros_like(o_ref)

    @pl.when((t < nv_ref[0]) & ((t == 0) | (blk_ref[t] != blk_ref[jnp.maximum(t - 1, 0)])))
    def _():
        for cb in range(2 * D_FF // PAIR_BLK):
            cs = slice(cb * PAIR_BLK, (cb + 1) * PAIR_BLK)
            wu_s[:, cs] = jnp.dot(wu_ref[0, 0, :, cs].astype(BF16), perm_ref[...],
                                  preferred_element_type=F32).astype(BF16)
        wd_s[...] = wd_ref[0, 0].astype(BF16)

    @pl.when(t < nv_ref[0])
    def _():
        x = xs_ref[...].astype(BF16)
        h = jnp.dot(x, wu_s[...], preferred_element_type=F32) + bu_ref[0]
        acts = []
        for cb in range(2 * D_FF // PAIR_BLK):
            glu = jnp.minimum(h[:, cb * PAIR_BLK:cb * PAIR_BLK + half], SWIGLU_LIMIT)
            lin = jnp.clip(h[:, cb * PAIR_BLK + half:(cb + 1) * PAIR_BLK], -SWIGLU_LIMIT, SWIGLU_LIMIT)
            acts.append(glu * _sigmoid(SWIGLU_ALPHA * glu) * (lin + 1.0))
        act = jnp.concatenate(acts, axis=1)
        o_ref[...] = jnp.dot(act.astype(BF16), wd_s[...], preferred_element_type=F32) + bd_ref[0]


def _experts(xs, blk_expert, n_valid, layer, w_up, b_up_split, w_down, b_down):
    rows = xs.shape[0]
    tm = ROW_TILE_EXPERT
    tile = lambda i, blk, nv: (jnp.minimum(i, nv[0] - 1), 0)
    wsel = lambda i, blk, nv: (blk[i], 0, 0)
    wsel4 = lambda i, blk, nv: (layer, blk[i], 0, 0)
    grid_spec = pltpu.PrefetchScalarGridSpec(
        num_scalar_prefetch=2,
        grid=(rows // tm,),
        in_specs=[
            pl.BlockSpec((tm, D_MODEL), tile),
            pl.BlockSpec((1, 1, D_MODEL, 2 * D_FF), wsel4),
            pl.BlockSpec((1, 1, 2 * D_FF), wsel),
            pl.BlockSpec((1, 1, D_FF, D_MODEL), wsel4),
            pl.BlockSpec((1, 1, D_MODEL), wsel),
            pl.BlockSpec((PAIR_BLK, PAIR_BLK), lambda i, blk, nv: (0, 0)),
        ],
        out_specs=pl.BlockSpec((tm, D_MODEL), lambda i, blk, nv: (i, 0)),
        scratch_shapes=[pltpu.VMEM((D_MODEL, 2 * D_FF), BF16), pltpu.VMEM((D_FF, D_MODEL), BF16)],
    )
    return pl.pallas_call(
        _expert_kernel,
        grid_spec=grid_spec,
        out_shape=jax.ShapeDtypeStruct((rows, D_MODEL), F32),
        compiler_params=_cparams("arbitrary"),
        name="experts",
    )(blk_expert, n_valid, xs, w_up, b_up_split, w_down, b_down, _pair_split_matrix())


def _combine_kernel(dest_ref, dnext_ref, gate_ref, x_ref, gn_ref, ys_hbm, *rest, final_norm, split_tiles):
    if split_tiles is None:
        o_ref, buf, sem = rest
    else:
        op_ref, os_ref, buf, sem = rest
    i = pl.program_id(0)
    n = pl.num_programs(0)
    tm = ROW_TILE_MOVE

    def row_copy(d, k, r, slot):
        return pltpu.make_async_copy(ys_hbm.at[pl.ds(d, 1)], buf.at[slot, k, pl.ds(r, 1)], sem.at[slot])

    def issue(idx_ref, slot):
        def body(r, carry):
            for k in range(TOP_K):
                row_copy(idx_ref[r * TOP_K + k], k, r, slot).start(priority=k % 2)
            return carry
        lax.fori_loop(0, tm, body, 0, unroll=4)

    @pl.when(i == 0)
    def _():
        issue(dest_ref, 0)

    @pl.when(i + 1 < n)
    def _():
        issue(dnext_ref, (i + 1) % 2)

    slot = i % 2

    for _ in range(tm * TOP_K):
        row_copy(0, 0, 0, slot).wait()

    gate = gate_ref[...]
    y = x_ref[...]
    for k in range(TOP_K):
        y = y + gate[:, k:k + 1] * buf[slot, k]
    if final_norm:
        ms = jnp.mean(y * y, axis=-1, keepdims=True)
        y = (y * lax.rsqrt(ms + NORM_EPS)) * gn_ref[...]
    if split_tiles is None:
        o_ref[...] = y
    else:
        @pl.when(i < split_tiles)
        def _():
            op_ref[...] = y

        @pl.when(i >= split_tiles)
        def _():
            os_ref[...] = y


def _combine(x, ys, dest_flat, gates, gain, final_norm, split_rows=None):
    t = x.shape[0]
    tm = ROW_TILE_MOVE
    nt = t // tm
    if split_rows is None:
        split_tiles = None
        out_specs = pl.BlockSpec((tm, D_MODEL), lambda i: (i, 0))
        out_shape = jax.ShapeDtypeStruct((t, D_MODEL), F32)
    else:
        split_tiles = split_rows // tm
        assert split_rows % tm == 0 and 0 < split_tiles < nt
        out_specs = [pl.BlockSpec((tm, D_MODEL), lambda i: (jnp.minimum(i, split_tiles - 1), 0)),
                     pl.BlockSpec((tm, D_MODEL), lambda i: (jnp.maximum(i - split_tiles, 0), 0))]
        out_shape = [jax.ShapeDtypeStruct((split_rows, D_MODEL), F32),
                     jax.ShapeDtypeStruct((t - split_rows, D_MODEL), F32)]
    kern = functools.partial(_combine_kernel, final_norm=final_norm, split_tiles=split_tiles)
    return pl.pallas_call(
        kern,
        grid=(nt,),
        in_specs=[
            pl.BlockSpec((tm * TOP_K,), lambda i: (i,), memory_space=pltpu.SMEM),
            pl.BlockSpec((tm * TOP_K,), lambda i: (jnp.minimum(i + 1, nt - 1),), memory_space=pltpu.SMEM),
            pl.BlockSpec((tm, LANES), lambda i: (i, 0)),
            pl.BlockSpec((tm, D_MODEL), lambda i: (i, 0)),
            pl.BlockSpec((1, D_MODEL), lambda i: (0, 0)),
            pl.BlockSpec(memory_space=pl.ANY),
        ],
        out_specs=out_specs,
        out_shape=out_shape,
        scratch_shapes=[pltpu.VMEM((2, TOP_K, tm, D_MODEL), F32), pltpu.SemaphoreType.DMA((2,))],
        compiler_params=_cparams("arbitrary"),
        name="combine",
    )(dest_flat, dest_flat, gates, x, gain.reshape(1, D_MODEL), ys)


def _sorted_rows_buffer(t):
    tm = ROW_TILE_EXPERT
    n_tiles = (t * TOP_K + N_EXPERTS * (tm - 1) + tm - 1) // tm
    return jnp.zeros((n_tiles * tm, D_MODEL), F32)


def _moe(x, h, logits, layer, w_up, b_up, w_down, b_down, gain, final_norm, xs_buf, split_rows=None):
    t = x.shape[0]
    tm = ROW_TILE_EXPERT
    eidx, gates, rank, counts = _route(logits)
    counts = counts[0, :N_EXPERTS].astype(I32)
    padded = (counts + tm - 1) // tm * tm
    pad_end = jnp.cumsum(padded)
    pad_start = pad_end - padded
    dest = (pad_start[eidx[:, :TOP_K]] + rank[:, :TOP_K]).reshape(-1)
    n_tiles = (t * TOP_K + N_EXPERTS * (tm - 1) + tm - 1) // tm
    tile_start = jnp.arange(n_tiles, dtype=I32) * tm
    blk_expert = jnp.minimum(jnp.sum((pad_end[None, :] <= tile_start[:, None]).astype(I32), axis=1), N_EXPERTS - 1)
    n_valid = (pad_end[-1:] // tm).astype(I32)
    xs = _scatter_rows(h, dest, xs_buf)
    ys = _experts(xs, blk_expert, n_valid, layer, w_up, b_up, w_down, b_down)
    return _combine(x, ys, dest, gates, gain, final_norm, split_rows), xs


def kernel(x_prompt, x_sample, cache_k, cache_v, state_hgrn, norm_mix, w_in, attn_sinks, lb_logits,
           hgrn_norm, w_out, norm_ffn, w_router, b_router, w_up, b_up, w_down, b_down, norm_final):
    bsz, seq, _ = x_prompt.shape
    n_seq, n_new, _ = x_sample.shape
    depth = w_in.shape[0]
    rows = cache_k.shape[2]
    tp = bsz * seq

    lb_soft = jax.nn.softmax(lb_logits.astype(F32), axis=0)
    lower_bounds = jnp.cumsum(lb_soft, axis=0) - lb_soft[0:1]

    cuts = np.cumsum([0, D_MODEL, KV_WIDTH, KV_WIDTH, D_MODEL, D_MODEL, D_MODEL, D_MODEL, D_MODEL, D_MODEL])
    part = lambda s: w_in[:, :, cuts[s]:cuts[s + 1]]
    w_in_p = jnp.concatenate([part(0), part(3), part(4), part(5), part(6), part(7), part(8), part(1), part(2)],
                             axis=-1).astype(BF16)
    w_out_b = w_out.astype(BF16)
    b_up_p = _split_pairs(b_up).reshape(depth, N_EXPERTS, 1, 2 * D_FF)
    b_down_r = b_down.reshape(depth, N_EXPERTS, 1, D_MODEL)
    w_router_p = jnp.pad(w_router, ((0, 0), (0, 0), (0, LANES - N_EXPERTS))).astype(BF16)
    b_router_p = jnp.pad(b_router, ((0, 0), (0, LANES - N_EXPERTS))).reshape(depth, 1, LANES)

    tab_p = _rope_tables(jnp.arange(seq, dtype=I32))
    tab_s = _rope_tables(PAST_LEN + jnp.arange(n_new, dtype=I32))

    x = jnp.concatenate([x_prompt.reshape(tp, D_MODEL), x_sample.reshape(n_seq * n_new, D_MODEL)], axis=0)
    kp_l, vp_l, sp_l, ks_l, vs_l, ss_l = [], [], [], [], [], []
    xs_buf = _sorted_rows_buffer(x.shape[0])
    for l in range(depth):
        z = _inproj(x, norm_mix[l], w_in_p[l])
        zs = z[tp:].reshape(n_seq, n_new, IN_DIM)
        a_p, kp, vp = _attn_prompt(z, attn_sinks[l], tab_p, bsz, seq)
        a_s, kx, vx = _attn_sample(zs, attn_sinks[l], tab_s,
                                   cache_k[l].reshape(n_seq, rows, KV_WIDTH),
                                   cache_v[l].reshape(n_seq, rows, KV_WIDTH))
        b_p, sp = _hgrn_prompt(z, lower_bounds[l], hgrn_norm[l], bsz, seq)
        b_s, sx = _hgrn_sample(zs, state_hgrn[l].astype(F32), lower_bounds[l], hgrn_norm[l])
        x_mid, h, logits = _merge(z, a_p, a_s.reshape(-1, D_MODEL), b_p, b_s.reshape(-1, D_MODEL), x,
                                  w_out_b[l], norm_ffn[l], w_router_p[l], b_router_p[l])
        last = l == depth - 1
        x, xs_buf = _moe(x_mid, h, logits, l, w_up, b_up_p[l], w_down, b_down_r[l], norm_final, last, xs_buf,
                         split_rows=tp if last else None)
        kp_l.append(kp.reshape(bsz, WINDOW, KV_HEADS, HEAD_DIM))
        vp_l.append(vp.reshape(bsz, WINDOW, KV_HEADS, HEAD_DIM))
        sp_l.append(jnp.swapaxes(sp, -1, -2))
        ks_l.append(kx.reshape(n_seq, rows, KV_HEADS, HEAD_DIM))
        vs_l.append(vx.reshape(n_seq, rows, KV_HEADS, HEAD_DIM))
        ss_l.append(sx)
    y_prompt = x[0].reshape(bsz, seq, D_MODEL)
    y_sample = x[1].reshape(n_seq, n_new, D_MODEL)
    return (y_prompt, y_sample, jnp.stack(kp_l), jnp.stack(vp_l), jnp.stack(sp_l).astype(x_prompt.dtype),
            jnp.stack(ks_l), jnp.stack(vs_l), jnp.stack(ss_l).astype(state_hgrn.dtype))
```

```python
import functools

import numpy as np
import jax
import jax.numpy as jnp
from jax import lax
from jax.experimental import pallas as pl
from jax.experimental.pallas import tpu as pltpu

F32 = jnp.float32
BF16 = jnp.bfloat16
I32 = jnp.int32

D_MODEL = 1024
HEAD_DIM = 64
A_HEADS = 16
KV_HEADS = 4
GROUP = 4
KV_WIDTH = KV_HEADS * HEAD_DIM
WINDOW = 128
PAST_LEN = 8192
ROT_DIM = 16
ROPE_THETA = 500000.0
B_KEY = 128
B_VAL = 128
B_HEADS = 8
CHUNK = 64
N_EXPERTS = 32
TOP_K = 4
D_FF = 1024
SWIGLU_ALPHA = 1.702
SWIGLU_LIMIT = 7.0
NORM_EPS = 1e-5
LANES = 128
PAIR_BLK = 2 * LANES

SEG_QA, SEG_QB, SEG_FB, SEG_IB, SEG_OG, SEG_GA, SEG_GB = range(7)
KV_BLK_K = 7 * D_MODEL // KV_WIDTH
KV_BLK_V = KV_BLK_K + 1
IN_DIM = 7 * D_MODEL + 2 * KV_WIDTH

ROW_TILE_IN = 512
COL_TILE_IN = 2560
ROW_TILE_OUT = 256
ROW_TILE_ROUTE = 512
ROW_TILE_MOVE = 128
ROW_TILE_EXPERT = 256
SEQS_PER_STEP = 8
HGRN_CHUNKS_PER_STEP = 4

VMEM_LIMIT = 56 * 1024 * 1024


def _cparams(*sem):
    return pltpu.CompilerParams(dimension_semantics=sem, vmem_limit_bytes=VMEM_LIMIT)


def _tile(n, pref):
    t = pref
    while t > LANES and n % t:
        t //= 2
    assert n % t == 0, (n, pref)
    return t


def _sigmoid(x):
    return 1.0 / (1.0 + jnp.exp(-x))


def _silu(x):
    return x * _sigmoid(x)


def _inproj_kernel(x_ref, g_ref, w_ref, z_ref, h_scr):
    @pl.when(pl.program_id(1) == 0)
    def _():
        x = x_ref[...]
        ms = jnp.mean(x * x, axis=-1, keepdims=True)
        h_scr[...] = ((x * lax.rsqrt(ms + NORM_EPS)) * g_ref[...]).astype(BF16)

    z_ref[...] = jnp.dot(h_scr[...], w_ref[...], preferred_element_type=F32)


def _inproj(x, gain, w_bf16):
    t = x.shape[0]
    tm, tn = _tile(t, ROW_TILE_IN), COL_TILE_IN
    return pl.pallas_call(
        _inproj_kernel,
        grid=(t // tm, IN_DIM // tn),
        in_specs=[
            pl.BlockSpec((tm, D_MODEL), lambda i, j: (i, 0)),
            pl.BlockSpec((1, D_MODEL), lambda i, j: (0, 0)),
            pl.BlockSpec((D_MODEL, tn), lambda i, j: (0, j)),
        ],
        out_specs=pl.BlockSpec((tm, tn), lambda i, j: (i, j)),
        out_shape=jax.ShapeDtypeStruct((t, IN_DIM), F32),
        scratch_shapes=[pltpu.VMEM((tm, D_MODEL), BF16)],
        compiler_params=_cparams("arbitrary", "arbitrary"),
        name="inproj",
    )(x, gain.reshape(1, D_MODEL), w_bf16)


def _rope_tables(pos):
    half = ROT_DIM // 2
    inv_freq = ROPE_THETA ** (-2.0 * jnp.arange(half, dtype=F32) / ROT_DIM)
    ang = pos.astype(F32)[:, None] * inv_freq[None, :]
    cos, sin = jnp.cos(ang), jnp.sin(ang)
    n = pos.shape[0]
    rest = HEAD_DIM - ROT_DIM
    c_head = jnp.concatenate([cos, cos, jnp.ones((n, rest), F32)], axis=1)
    a_head = jnp.concatenate([-sin, jnp.zeros((n, half + rest), F32)], axis=1)
    b_head = jnp.concatenate([jnp.zeros((n, half), F32), sin, jnp.zeros((n, rest), F32)], axis=1)
    reps = LANES // HEAD_DIM
    return (jnp.tile(c_head, (1, reps)), jnp.tile(a_head, (1, reps)), jnp.tile(b_head, (1, reps)))


def _rope_slab(x, c, a, b):
    half = ROT_DIM // 2
    up = pltpu.roll(x, LANES - half, 1)
    dn = pltpu.roll(x, half, 1)
    return x * c + up * a + dn * b


def _rope(x, c, a, b):
    slabs = [_rope_slab(x[:, s * LANES:(s + 1) * LANES], c, a, b) for s in range(x.shape[1] // LANES)]
    return jnp.concatenate(slabs, axis=1)


def _attn_prompt_kernel(sink_ref, q_ref, k_ref, v_ref, c_ref, a_ref, b_ref,
                        o_ref, kn_ref, vn_ref, kprev, vprev):
    i = pl.program_id(1)
    w = WINDOW

    @pl.when(i == 0)
    def _():
        kprev[...] = jnp.zeros_like(kprev)
        vprev[...] = jnp.zeros_like(vprev)

    c, a, b = c_ref[...], a_ref[...], b_ref[...]
    q = (_rope(q_ref[...], c, a, b) * (HEAD_DIM ** -0.5)).astype(BF16)
    k_rot = _rope(k_ref[...], c, a, b)
    v_cur = v_ref[...]
    kn_ref[0] = k_rot
    vn_ref[0] = v_cur
    k_cur = k_rot.astype(BF16)
    v_curb = v_cur.astype(BF16)
    gw = GROUP * w
    row = lax.broadcasted_iota(I32, (gw, w), 0)
    r = row & (w - 1)
    col = lax.broadcasted_iota(I32, (gw, w), 1)
    valid_p = (col >= jnp.where(i > 0, r, w))
    valid_c = col <= r
    head_of_row = lax.broadcasted_iota(I32, (gw, 1), 0) // w
    nt = (((1,), (1,)), ((), ()))

    for j in range(KV_HEADS):
        js = slice(j * HEAD_DIM, (j + 1) * HEAD_DIM)
        qj = jnp.concatenate([q[:, (j * GROUP + g) * HEAD_DIM:(j * GROUP + g + 1) * HEAD_DIM]
                              for g in range(GROUP)], axis=0)
        sink = jnp.zeros((gw, 1), F32)
        for g in range(GROUP):
            sink = jnp.where(head_of_row == g, sink_ref[j * GROUP + g], sink)
        s_p = jnp.where(valid_p, lax.dot_general(qj, kprev[j], nt, preferred_element_type=F32), -jnp.inf)
        s_c = jnp.where(valid_c, lax.dot_general(qj, k_cur[:, js], nt, preferred_element_type=F32), -jnp.inf)
        m = jnp.maximum(jnp.maximum(jnp.max(s_p, axis=-1, keepdims=True),
                                    jnp.max(s_c, axis=-1, keepdims=True)), sink)
        p_p = jnp.exp(s_p - m)
        p_c = jnp.exp(s_c - m)
        denom = (jnp.sum(p_p, axis=-1, keepdims=True) + jnp.sum(p_c, axis=-1, keepdims=True)
                 + jnp.exp(sink - m))
        inv = 1.0 / denom
        o = (jnp.dot((p_p * inv).astype(BF16), vprev[j], preferred_element_type=F32)
             + jnp.dot((p_c * inv).astype(BF16), v_curb[:, js], preferred_element_type=F32))
        for g in range(GROUP):
            h = j * GROUP + g
            o_ref[:, h * HEAD_DIM:(h + 1) * HEAD_DIM] = o[g * w:(g + 1) * w, :]

    for j in range(KV_HEADS):
        kprev[j] = k_cur[:, j * HEAD_DIM:(j + 1) * HEAD_DIM]
        vprev[j] = v_curb[:, j * HEAD_DIM:(j + 1) * HEAD_DIM]


def _attn_prompt(z, sinks, tables, bsz, seq):
    nb = seq // WINDOW
    w = WINDOW
    row = lambda b, i: b * nb + i
    return pl.pallas_call(
        _attn_prompt_kernel,
        grid=(bsz, nb),
        in_specs=[
            pl.BlockSpec(memory_space=pltpu.SMEM),
            pl.BlockSpec((w, D_MODEL), lambda b, i: (row(b, i), SEG_QA)),
            pl.BlockSpec((w, KV_WIDTH), lambda b, i: (row(b, i), KV_BLK_K)),
            pl.BlockSpec((w, KV_WIDTH), lambda b, i: (row(b, i), KV_BLK_V)),
            pl.BlockSpec((w, LANES), lambda b, i: (i, 0)),
            pl.BlockSpec((w, LANES), lambda b, i: (i, 0)),
            pl.BlockSpec((w, LANES), lambda b, i: (i, 0)),
        ],
        out_specs=[
            pl.BlockSpec((w, D_MODEL), lambda b, i: (row(b, i), 0)),
            pl.BlockSpec((1, w, KV_WIDTH), lambda b, i: (b, 0, 0)),
            pl.BlockSpec((1, w, KV_WIDTH), lambda b, i: (b, 0, 0)),
        ],
        out_shape=[
            jax.ShapeDtypeStruct((bsz * seq, D_MODEL), F32),
            jax.ShapeDtypeStruct((bsz, w, KV_WIDTH), F32),
            jax.ShapeDtypeStruct((bsz, w, KV_WIDTH), F32),
        ],
        scratch_shapes=[pltpu.VMEM((KV_HEADS, w, HEAD_DIM), BF16), pltpu.VMEM((KV_HEADS, w, HEAD_DIM), BF16)],
        compiler_params=_cparams("arbitrary", "arbitrary"),
        name="attn_prompt",
    )(sinks, z, z, z, *tables)


def _attn_sample_kernel(sink_ref, q_ref, k_ref, v_ref, kc_ref, vc_ref, c_ref, a_ref, b_ref,
                        o_ref, kn_ref, vn_ref, *, n_new):
    nsq = SEQS_PER_STEP
    rows = kc_ref.shape[1]
    c, a, b = c_ref[...], a_ref[...], b_ref[...]
    tq = lax.broadcasted_iota(I32, (nsq, n_new, rows), 1)
    cc = lax.broadcasted_iota(I32, (nsq, n_new, rows), 2)
    valid_c = cc >= tq
    tn = lax.broadcasted_iota(I32, (nsq, n_new, n_new), 1)
    nn = lax.broadcasted_iota(I32, (nsq, n_new, n_new), 2)
    valid_n = nn <= tn
    qs, kns, vns = [], [], []
    for sq in range(nsq):
        qs.append((_rope(q_ref[sq], c, a, b) * (HEAD_DIM ** -0.5)).astype(BF16))
        k_new = _rope(k_ref[sq], c, a, b)
        v_new = v_ref[sq]
        kn_ref[sq, 0:rows - n_new, :] = kc_ref[sq, n_new:rows, :]
        kn_ref[sq, rows - n_new:rows, :] = k_new
        vn_ref[sq, 0:rows - n_new, :] = vc_ref[sq, n_new:rows, :]
        vn_ref[sq, rows - n_new:rows, :] = v_new
        kns.append(k_new.astype(BF16))
        vns.append(v_new.astype(BF16))
    q3, kn3, vn3 = jnp.stack(qs), jnp.stack(kns), jnp.stack(vns)
    kc3, vc3 = kc_ref[...].astype(BF16), vc_ref[...].astype(BF16)
    qk = (((2,), (2,)), ((0,), (0,)))
    pv = (((2,), (1,)), ((0,), (0,)))
    outs = []
    for h in range(A_HEADS):
        j = h // GROUP
        hs = slice(h * HEAD_DIM, (h + 1) * HEAD_DIM)
        js = slice(j * HEAD_DIM, (j + 1) * HEAD_DIM)
        qh = q3[:, :, hs]
        s_c = jnp.where(valid_c, lax.dot_general(qh, kc3[:, :, js], qk, preferred_element_type=F32), -jnp.inf)
        s_n = jnp.where(valid_n, lax.dot_general(qh, kn3[:, :, js], qk, preferred_element_type=F32), -jnp.inf)
        sink = sink_ref[h]
        m = jnp.maximum(jnp.maximum(jnp.max(s_c, axis=-1, keepdims=True),
                                    jnp.max(s_n, axis=-1, keepdims=True)), sink)
        p_c = jnp.exp(s_c - m)
        p_n = jnp.exp(s_n - m)
        denom = (jnp.sum(p_c, axis=-1, keepdims=True) + jnp.sum(p_n, axis=-1, keepdims=True)
                 + jnp.exp(sink - m))
        inv = 1.0 / denom
        outs.append(lax.dot_general((p_c * inv).astype(BF16), vc3[:, :, js], pv, preferred_element_type=F32)
                    + lax.dot_general((p_n * inv).astype(BF16), vn3[:, :, js], pv, preferred_element_type=F32))
    o_ref[...] = jnp.concatenate(outs, axis=2)


def _attn_sample(zs, sinks, tables, cache_k, cache_v):
    n_seq, n_new, _ = zs.shape
    rows = cache_k.shape[1]
    g = SEQS_PER_STEP
    kern = functools.partial(_attn_sample_kernel, n_new=n_new)
    return pl.pallas_call(
        kern,
        grid=(n_seq // g,),
        in_specs=[
            pl.BlockSpec(memory_space=pltpu.SMEM),
            pl.BlockSpec((g, n_new, D_MODEL), lambda i: (i, 0, SEG_QA)),
            pl.BlockSpec((g, n_new, KV_WIDTH), lambda i: (i, 0, KV_BLK_K)),
            pl.BlockSpec((g, n_new, KV_WIDTH), lambda i: (i, 0, KV_BLK_V)),
            pl.BlockSpec((g, rows, KV_WIDTH), lambda i: (i, 0, 0)),
            pl.BlockSpec((g, rows, KV_WIDTH), lambda i: (i, 0, 0)),
            pl.BlockSpec((n_new, LANES), lambda i: (0, 0)),
            pl.BlockSpec((n_new, LANES), lambda i: (0, 0)),
            pl.BlockSpec((n_new, LANES), lambda i: (0, 0)),
        ],
        out_specs=[
            pl.BlockSpec((g, n_new, D_MODEL), lambda i: (i, 0, 0)),
            pl.BlockSpec((g, rows, KV_WIDTH), lambda i: (i, 0, 0)),
            pl.BlockSpec((g, rows, KV_WIDTH), lambda i: (i, 0, 0)),
        ],
        out_shape=[
            jax.ShapeDtypeStruct((n_seq, n_new, D_MODEL), F32),
            jax.ShapeDtypeStruct(cache_k.shape, F32),
            jax.ShapeDtypeStruct(cache_v.shape, F32),
        ],
        compiler_params=_cparams("arbitrary"),
        name="attn_sample",
    )(sinks, zs, zs, zs, cache_k, cache_v, *tables)


def _hgrn_levels(c):
    lv = []
    hb = c // 2
    while hb >= 1:
        lv.append(hb)
        hb //= 2
    return lv


def _hgrn_prompt_kernel(qb_ref, fb_ref, ib_ref, og_ref, lb_ref, hn_ref, tri_ref,
                        o_ref, st_ref, g_scr, state, *, chunks_per_step):
    step = pl.program_id(1)
    c = CHUNK

    @pl.when(step == 0)
    def _():
        state[...] = jnp.zeros_like(state)

    lb = lb_ref[...]
    rt = lax.broadcasted_iota(I32, (c, c), 0)
    cs = lax.broadcasted_iota(I32, (c, c), 1)
    masks = []
    for hb in _hgrn_levels(c):
        same = (rt // (2 * hb)) == (cs // (2 * hb))
        masks.append(same & ((rt // hb) % 2 == 1) & ((cs // hb) % 2 == 0))
    row_t = lax.broadcasted_iota(I32, (8, LANES), 0)

    def one_chunk(ci, carry):
        rows = pl.ds(pl.multiple_of(ci * c, c), c)
        f = lb + (1.0 - lb) * _sigmoid(fb_ref[rows, :])
        logf = jnp.log(f)
        g_scr[...] = jnp.dot(tri_ref[...], logf, preferred_element_type=F32, precision=lax.Precision.HIGHEST)

        for h in range(B_HEADS):
            hs = slice(h * B_KEY, (h + 1) * B_KEY)
            qh = _silu(qb_ref[rows, hs]) * (B_KEY ** -0.5)
            kk = 1.0 - f[:, hs]
            v = ib_ref[rows, hs]
            vb = v.astype(BF16)
            g = g_scr[:, hs]
            scores = jnp.zeros((c, c), F32)

            def bcast_row(r, hs=hs):
                return jnp.broadcast_to(g_scr[pl.ds(r, 1), hs], (8, B_KEY))

            for lvl, hb in enumerate(_hgrn_levels(c)):
                tiles = []
                for tix in range(c // 8):
                    base = tix * 8
                    if hb >= 8:
                        tiles.append(bcast_row((base // (2 * hb)) * 2 * hb + hb - 1))
                    else:
                        refs = sorted({(tt // (2 * hb)) * 2 * hb + hb - 1 for tt in range(8)})
                        cur = bcast_row(base + refs[0])
                        for rr in refs[1:]:
                            cur = jnp.where(row_t >= (rr - hb + 1), bcast_row(base + rr), cur)
                        tiles.append(cur)
                ref = jnp.concatenate(tiles, axis=0)
                e = jnp.exp(-jnp.abs(g - ref))
                qs = (qh * e).astype(BF16)
                ks = (kk * e).astype(BF16)
                part = lax.dot_general(qs, ks, (((1,), (1,)), ((), ())), preferred_element_type=F32)
                scores = scores + jnp.where(masks[lvl], part, 0.0)
            diag = jnp.sum(qh * kk, axis=-1, keepdims=True)
            o_intra = jnp.dot(scores.astype(BF16), vb, preferred_element_type=F32) + diag * v
            st = state[h]
            qg = (qh * jnp.exp(g)).astype(BF16)
            o_inter = lax.dot_general(qg, st.astype(BF16), (((1,), (1,)), ((), ())), preferred_element_type=F32)
            g_end = g[c - 1:c, :]
            kd = (kk * jnp.exp(g_end - g)).astype(BF16)
            upd = lax.dot_general(vb, kd, (((0,), (0,)), ((), ())), preferred_element_type=F32)
            state[h] = jnp.exp(g_end) * st + upd
            o = o_inter + o_intra
            ms = jnp.mean(o * o, axis=-1, keepdims=True)
            o = (o * lax.rsqrt(ms + NORM_EPS)) * hn_ref[...]
            o_ref[rows, hs] = o * _silu(og_ref[rows, hs])
        return carry

    lax.fori_loop(0, chunks_per_step, one_chunk, 0)

    @pl.when(step == pl.num_programs(1) - 1)
    def _():
        st_ref[0] = state[...]


def _tri(c):
    return jnp.asarray(np.tril(np.ones((c, c), np.float32)))


def _hgrn_prompt(z, lower_bound, hgrn_norm, bsz, seq):
    c = CHUNK
    cps = HGRN_CHUNKS_PER_STEP
    rows = c * cps
    nc = seq // rows
    row = lambda b, i: b * nc + i
    seg = lambda s: pl.BlockSpec((rows, D_MODEL), lambda b, i: (row(b, i), s))
    return pl.pallas_call(
        functools.partial(_hgrn_prompt_kernel, chunks_per_step=cps),
        grid=(bsz, nc),
        in_specs=[
            seg(SEG_QB), seg(SEG_FB), seg(SEG_IB), seg(SEG_OG),
            pl.BlockSpec((1, D_MODEL), lambda b, i: (0, 0)),
            pl.BlockSpec((1, B_VAL), lambda b, i: (0, 0)),
            pl.BlockSpec((c, c), lambda b, i: (0, 0)),
        ],
        out_specs=[
            pl.BlockSpec((rows, D_MODEL), lambda b, i: (row(b, i), 0)),
            pl.BlockSpec((1, B_HEADS, B_VAL, B_KEY), lambda b, i: (b, 0, 0, 0)),
        ],
        out_shape=[
            jax.ShapeDtypeStruct((bsz * seq, D_MODEL), F32),
            jax.ShapeDtypeStruct((bsz, B_HEADS, B_VAL, B_KEY), F32),
        ],
        scratch_shapes=[pltpu.VMEM((c, D_MODEL), F32), pltpu.VMEM((B_HEADS, B_VAL, B_KEY), F32)],
        compiler_params=_cparams("arbitrary", "arbitrary"),
        name="hgrn_prompt",
    )(z, z, z, z, lower_bound.reshape(1, D_MODEL), hgrn_norm.reshape(1, B_VAL), _tri(c))


def _hgrn_sample_kernel(qb_ref, fb_ref, ib_ref, og_ref, s_ref, lb_ref, hn_ref,
                        o_ref, sn_ref, *, n_new):
    lb = lb_ref[...]
    rows_t = lax.broadcasted_iota(I32, (n_new, 1), 0)
    for sq in range(SEQS_PER_STEP):
        f = lb + (1.0 - lb) * _sigmoid(fb_ref[sq])
        logf = jnp.log(f)
        g_rows = [logf[0:1, :]]
        for t in range(1, n_new):
            g_rows.append(g_rows[-1] + logf[t:t + 1, :])
        g_all = jnp.concatenate(g_rows, axis=0)
        qh_all = _silu(qb_ref[sq]) * (B_KEY ** -0.5)
        kk_all = 1.0 - f
        v_all = ib_ref[sq]
        og_all = og_ref[sq]
        outs = []
        for h in range(B_HEADS):
            hs = slice(h * B_KEY, (h + 1) * B_KEY)
            g, qh, kk, v = g_all[:, hs], qh_all[:, hs], kk_all[:, hs], v_all[:, hs]
            s0 = s_ref[sq, h]
            o = jnp.dot((qh * jnp.exp(g)).astype(BF16), s0.astype(BF16), preferred_element_type=F32)
            for s in range(n_new):
                diff = jnp.where(rows_t >= s, g - g[s:s + 1, :], 0.0)
                sc = jnp.sum(qh * kk[s:s + 1, :] * jnp.exp(diff), axis=-1, keepdims=True)
                sc = jnp.where(rows_t >= s, sc, 0.0)
                o = o + sc * v[s:s + 1, :]
            g_end = g[n_new - 1:n_new, :]
            kd = (kk * jnp.exp(g_end - g)).astype(BF16)
            upd = lax.dot_general(kd, v.astype(BF16), (((0,), (0,)), ((), ())), preferred_element_type=F32)
            decay = jnp.transpose(jnp.broadcast_to(jnp.exp(g_end), (B_VAL, B_KEY)))
            sn_ref[sq, h] = decay * s0 + upd
            ms = jnp.mean(o * o, axis=-1, keepdims=True)
            o = (o * lax.rsqrt(ms + NORM_EPS)) * hn_ref[...]
            outs.append(o * _silu(og_all[:, hs]))
        o_ref[sq] = jnp.concatenate(outs, axis=1)


def _hgrn_sample(zs, state, lower_bound, hgrn_norm):
    n_seq, n_new, _ = zs.shape
    g = SEQS_PER_STEP
    seg = lambda s: pl.BlockSpec((g, n_new, D_MODEL), lambda i: (i, 0, s))
    st_spec = pl.BlockSpec((g, B_HEADS, B_KEY, B_VAL), lambda i: (i, 0, 0, 0))
    kern = functools.partial(_hgrn_sample_kernel, n_new=n_new)
    return pl.pallas_call(
        kern,
        grid=(n_seq // g,),
        in_specs=[
            seg(SEG_QB), seg(SEG_FB), seg(SEG_IB), seg(SEG_OG), st_spec,
            pl.BlockSpec((1, D_MODEL), lambda i: (0, 0)),
            pl.BlockSpec((1, B_VAL), lambda i: (0, 0)),
        ],
        out_specs=[pl.BlockSpec((g, n_new, D_MODEL), lambda i: (i, 0, 0)), st_spec],
        out_shape=[jax.ShapeDtypeStruct((n_seq, n_new, D_MODEL), F32), jax.ShapeDtypeStruct(state.shape, F32)],
        compiler_params=_cparams("arbitrary"),
        name="hgrn_sample",
    )(zs, zs, zs, zs, state, lower_bound.reshape(1, D_MODEL), hgrn_norm.reshape(1, B_VAL))


def _merge_kernel(ga_ref, gb_ref, ap_ref, as_ref, bp_ref, bs_ref, x_ref, wo_ref, gn_ref, wr_ref, br_ref,
                  xn_ref, h_ref, lg_ref, *, prompt_tiles):
    is_prompt = pl.program_id(0) < prompt_tiles
    a = jnp.where(is_prompt, ap_ref[...], as_ref[...])
    b = jnp.where(is_prompt, bp_ref[...], bs_ref[...])
    merged = _sigmoid(ga_ref[...]) * a + _sigmoid(gb_ref[...]) * b
    x = x_ref[...] + jnp.dot(merged.astype(BF16), wo_ref[...], preferred_element_type=F32)
    xn_ref[...] = x
    ms = jnp.mean(x * x, axis=-1, keepdims=True)
    h = (x * lax.rsqrt(ms + NORM_EPS)) * gn_ref[...]
    h_ref[...] = h
    lg_ref[...] = jnp.dot(h.astype(BF16), wr_ref[...], preferred_element_type=F32) + br_ref[...]


def _merge(z, a_p, a_s, b_p, b_s, x, w_out_bf16, norm_ffn, w_router_pad, b_router_pad):
    t = x.shape[0]
    tp, ts = a_p.shape[0], a_s.shape[0]
    tm = _tile(np.gcd(tp, ts), ROW_TILE_OUT)
    npt = tp // tm
    rowspec = lambda w, cb: pl.BlockSpec((tm, w), lambda i: (i, cb))
    pspec = pl.BlockSpec((tm, D_MODEL), lambda i: (jnp.minimum(i, npt - 1), 0))
    sspec = pl.BlockSpec((tm, D_MODEL), lambda i: (jnp.maximum(i - npt, 0), 0))
    const = lambda shape: pl.BlockSpec(shape, lambda i: (0, 0))
    return pl.pallas_call(
        functools.partial(_merge_kernel, prompt_tiles=npt),
        grid=(t // tm,),
        in_specs=[
            rowspec(D_MODEL, SEG_GA), rowspec(D_MODEL, SEG_GB),
            pspec, sspec, pspec, sspec, rowspec(D_MODEL, 0),
            const((D_MODEL, D_MODEL)), const((1, D_MODEL)), const((D_MODEL, LANES)), const((1, LANES)),
        ],
        out_specs=[rowspec(D_MODEL, 0), rowspec(D_MODEL, 0), rowspec(LANES, 0)],
        out_shape=[
            jax.ShapeDtypeStruct((t, D_MODEL), F32),
            jax.ShapeDtypeStruct((t, D_MODEL), F32),
            jax.ShapeDtypeStruct((t, LANES), F32),
        ],
        compiler_params=_cparams("arbitrary"),
        name="merge_outproj",
    )(z, z, a_p, a_s, b_p, b_s, x, w_out_bf16, norm_ffn.reshape(1, D_MODEL), w_router_pad, b_router_pad)


def _route_kernel(lg_ref, tri_ref, eidx_ref, gate_ref, rank_ref, cnt_ref, carry):
    @pl.when(pl.program_id(0) == 0)
    def _():
        carry[...] = jnp.zeros_like(carry)

    tm = lg_ref.shape[0]
    lane = lax.broadcasted_iota(I32, (tm, LANES), 1)
    lane_f = lane.astype(F32)
    l = jnp.where(lane < N_EXPERTS, lg_ref[...], -jnp.inf)
    vals, idxs = [], []
    picked = jnp.zeros((tm, LANES), F32)
    for _ in range(TOP_K):
        m = jnp.max(l, axis=-1, keepdims=True)
        idx = jnp.min(jnp.where(l == m, lane_f, float(LANES)), axis=-1, keepdims=True).astype(I32)
        sel = lane == idx
        vals.append(m)
        idxs.append(idx)
        picked = picked + sel.astype(F32)
        l = jnp.where(sel, -jnp.inf, l)
    exps = [jnp.exp(v - vals[0]) for v in vals]
    total = exps[0]
    for e in exps[1:]:
        total = total + e
    inv = 1.0 / total
    before = jnp.dot(tri_ref[...], picked.astype(BF16), preferred_element_type=F32) + carry[...]
    eidx = jnp.zeros((tm, LANES), I32)
    gate = jnp.zeros((tm, LANES), F32)
    rank = jnp.zeros((tm, LANES), I32)
    for r in range(TOP_K):
        rk = jnp.sum(jnp.where(lane == idxs[r], before, 0.0), axis=-1, keepdims=True)
        eidx = jnp.where(lane == r, idxs[r], eidx)
        gate = jnp.where(lane == r, exps[r] * inv, gate)
        rank = jnp.where(lane == r, rk.astype(I32), rank)
    eidx_ref[...] = eidx
    gate_ref[...] = gate
    rank_ref[...] = rank
    carry[...] = carry[...] + jnp.sum(picked, axis=0, keepdims=True)
    cnt_ref[...] = carry[...]


def _route(logits):
    t = logits.shape[0]
    tm = _tile(t, ROW_TILE_ROUTE)
    tri =jnp.asarray(np.tril(np.ones((tm, tm), np.float32), -1)).astype(BF16)
    rows = pl.BlockSpec((tm, LANES), lambda i: (i, 0))
    return pl.pallas_call(
        _route_kernel,
        grid=(t // tm,),
        in_specs=[rows, pl.BlockSpec((tm, tm), lambda i: (0, 0))],
        out_specs=[rows, rows, rows, pl.BlockSpec((1, LANES), lambda i: (0, 0))],
        out_shape=[
            jax.ShapeDtypeStruct((t, LANES), I32),
            jax.ShapeDtypeStruct((t, LANES), F32),
            jax.ShapeDtypeStruct((t, LANES), I32),
            jax.ShapeDtypeStruct((1, LANES), F32),
        ],
        scratch_shapes=[pltpu.VMEM((1, LANES), F32)],
        compiler_params=_cparams("arbitrary"),
        name="route",
    )(logits, tri)


def _scatter_kernel(dest_ref, h_ref, xs_in, xs_hbm, sem):
    del xs_in
    tm = ROW_TILE_MOVE

    def row_copy(r, d):
        return pltpu.make_async_copy(h_ref.at[pl.ds(r, 1)], xs_hbm.at[pl.ds(d, 1)], sem.at[0])

    def issue(r, carry):
        for k in range(TOP_K):
            row_copy(r, dest_ref[r * TOP_K + k]).start(priority=k % 2)
        return carry

    lax.fori_loop(0, tm, issue, 0, unroll=4)

    for _ in range(tm * TOP_K):
        row_copy(0, 0).wait()


def _scatter_rows(h, dest_flat, xs_zero):
    t = h.shape[0]
    tm = ROW_TILE_MOVE
    return pl.pallas_call(
        _scatter_kernel,
        grid=(t // tm,),
        in_specs=[
            pl.BlockSpec((tm * TOP_K,), lambda i: (i,), memory_space=pltpu.SMEM),
            pl.BlockSpec((tm, D_MODEL), lambda i: (i, 0)),
            pl.BlockSpec(memory_space=pl.ANY),
        ],
        out_specs=pl.BlockSpec(memory_space=pl.ANY),
        out_shape=jax.ShapeDtypeStruct(xs_zero.shape, xs_zero.dtype),
        scratch_shapes=[pltpu.SemaphoreType.DMA((1,))],
        input_output_aliases={2: 0},
        compiler_params=_cparams("arbitrary"),
        name="scatter_rows",
    )(dest_flat, h, xs_zero)


def _pair_split_matrix():
    half = PAIR_BLK // 2
    p = np.zeros((PAIR_BLK, PAIR_BLK), np.float32)
    p[2 * np.arange(half), np.arange(half)] = 1.0
    p[2 * np.arange(half) + 1, half + np.arange(half)] = 1.0
    return jnp.asarray(p).astype(BF16)


def _split_pairs(v):
    lead = v.shape[:-1]
    v = v.reshape(lead + (v.shape[-1] // PAIR_BLK, PAIR_BLK // 2, 2))
    return jnp.swapaxes(v, -1, -2).reshape(lead + (-1,))


def _expert_kernel(blk_ref, nv_ref, first_ref, par_ref, nxt_ref, xs_ref, wu_hbm, bu_ref, wd_hbm, bd_ref, perm_ref,
                   o_ref, wu_s, wd_s, wu_f, wd_f, sem, *, layer):
    t = pl.program_id(0)
    half = PAIR_BLK // 2

    def fetch(expert, slot):
        return (pltpu.make_async_copy(wu_hbm.at[layer, expert], wu_f.at[slot], sem.at[0, slot]),
                pltpu.make_async_copy(wd_hbm.at[layer, expert], wd_f.at[slot], sem.at[1, slot]))

    @pl.when(t >= nv_ref[0])
    def _():
        o_ref[...] = jnp.zeros_like(o_ref)

    @pl.when(t == 0)
    def _():
        for cp in fetch(blk_ref[0], 0):
            cp.start()

    @pl.when(first_ref[t] == 1)
    def _():
        slot = par_ref[t]
        for cp in fetch(0, slot):
            cp.wait()

        @pl.when(nxt_ref[t] >= 0)
        def _():
            for cp in fetch(nxt_ref[t], 1 - slot):
                cp.start()

        for cb in range(2 * D_FF // PAIR_BLK):
            cs = slice(cb * PAIR_BLK, (cb + 1) * PAIR_BLK)
            wu_s[:, cs] = jnp.dot(wu_f[slot, :, cs].astype(BF16), perm_ref[...],
                                  preferred_element_type=F32).astype(BF16)
        wd_s[...] = wd_f[slot].astype(BF16)

    @pl.when(t < nv_ref[0])
    def _():
        x = xs_ref[...].astype(BF16)
        h = jnp.dot(x, wu_s[...], preferred_element_type=F32) + bu_ref[0]
        acts = []
        for cb in range(2 * D_FF // PAIR_BLK):
            glu = jnp.minimum(h[:, cb * PAIR_BLK:cb * PAIR_BLK + half], SWIGLU_LIMIT)
            lin = jnp.clip(h[:, cb * PAIR_BLK + half:(cb + 1) * PAIR_BLK], -SWIGLU_LIMIT, SWIGLU_LIMIT)
            acts.append(glu * _sigmoid(SWIGLU_ALPHA * glu) * (lin + 1.0))
        act = jnp.concatenate(acts, axis=1)
        o_ref[...] = jnp.dot(act.astype(BF16), wd_s[...], preferred_element_type=F32) + bd_ref[0]


def _experts(xs, blk_expert, n_valid, layer, w_up, b_up_split, w_down, b_down):
    rows = xs.shape[0]
    tm = ROW_TILE_EXPERT
    n_tiles = rows // tm
    tix = jnp.arange(n_tiles, dtype=I32)
    prev = jnp.concatenate([blk_expert[:1], blk_expert[:-1]])
    first = ((tix < n_valid[0]) & ((tix == 0) | (blk_expert != prev))).astype(I32)
    parity = ((jnp.cumsum(first) - 1) & 1).astype(I32)
    first_pos = jnp.where(first == 1, tix, n_tiles)
    later = jnp.concatenate([first_pos[1:], jnp.full((1,), n_tiles, I32)])
    nxt_pos = lax.cummin(later, axis=0, reverse=True)
    nxt = jnp.where(nxt_pos < n_tiles, blk_expert[jnp.minimum(nxt_pos, n_tiles - 1)], -1).astype(I32)

    tile = lambda i, *_: (jnp.minimum(i, _[1][0] - 1), 0)
    wsel = lambda i, *_: (_[0][i], 0, 0)
    grid_spec = pltpu.PrefetchScalarGridSpec(
        num_scalar_prefetch=5,
        grid=(n_tiles,),
        in_specs=[
            pl.BlockSpec((tm, D_MODEL), tile),
            pl.BlockSpec(memory_space=pl.ANY),
            pl.BlockSpec((1, 1, 2 * D_FF), wsel),
            pl.BlockSpec(memory_space=pl.ANY),
            pl.BlockSpec((1, 1, D_MODEL), wsel),
            pl.BlockSpec((PAIR_BLK, PAIR_BLK), lambda i, *_: (0, 0)),
        ],
        out_specs=pl.BlockSpec((tm, D_MODEL), lambda i, *_: (i, 0)),
        scratch_shapes=[pltpu.VMEM((D_MODEL, 2 * D_FF), BF16), pltpu.VMEM((D_FF, D_MODEL), BF16),
                        pltpu.VMEM((2, D_MODEL, 2 * D_FF), F32), pltpu.VMEM((2, D_FF, D_MODEL), F32),
                        pltpu.SemaphoreType.DMA((2, 2))],
    )
    return pl.pallas_call(
        functools.partial(_expert_kernel, layer=layer),
        grid_spec=grid_spec,
        out_shape=jax.ShapeDtypeStruct((rows, D_MODEL), F32),
        compiler_params=_cparams("arbitrary"),
        name="experts",
    )(blk_expert, n_valid, first, parity, nxt, xs, w_up, b_up_split, w_down, b_down, _pair_split_matrix())


def _combine_kernel(dest_ref, dnext_ref, gate_ref, x_ref, gn_ref, ys_hbm, *rest, final_norm, split_tiles):
    if split_tiles is None:
        o_ref, buf, sem = rest
    else:
        op_ref, os_ref, buf, sem = rest
    i = pl.program_id(0)
    n = pl.num_programs(0)
    tm = ROW_TILE_MOVE

    def row_copy(d, k, r, slot):
        return pltpu.make_async_copy(ys_hbm.at[pl.ds(d, 1)], buf.at[slot, k, pl.ds(r, 1)], sem.at[slot])

    def issue(idx_ref, slot):
        def body(r, carry):
            for k in range(TOP_K):
                row_copy(idx_ref[r * TOP_K + k], k, r, slot).start(priority=k % 2)
            return carry
        lax.fori_loop(0, tm, body, 0, unroll=4)

    @pl.when(i == 0)
    def _():
        issue(dest_ref, 0)

    @pl.when(i + 1 < n)
    def _():
        issue(dnext_ref, (i + 1) % 2)

    slot = i % 2

    for _ in range(tm * TOP_K):
        row_copy(0, 0, 0, slot).wait()

    gate = gate_ref[...]
    y = x_ref[...]
    for k in range(TOP_K):
        y = y + gate[:, k:k + 1] * buf[slot, k]
    if final_norm:
        ms = jnp.mean(y * y, axis=-1, keepdims=True)
        y = (y * lax.rsqrt(ms + NORM_EPS)) * gn_ref[...]
    if split_tiles is None:
        o_ref[...] = y
    else:
        @pl.when(i < split_tiles)
        def _():
            op_ref[...] = y

        @pl.when(i >= split_tiles)
        def _():
            os_ref[...] = y


def _combine(x, ys, dest_flat, gates, gain, final_norm, split_rows=None):
    t = x.shape[0]
    tm = ROW_TILE_MOVE
    nt = t // tm
    if split_rows is None:
        split_tiles = None
        out_specs = pl.BlockSpec((tm, D_MODEL), lambda i: (i, 0))
        out_shape = jax.ShapeDtypeStruct((t, D_MODEL), F32)
    else:
        split_tiles = split_rows // tm
        assert split_rows % tm == 0 and 0 < split_tiles < nt
        out_specs = [pl.BlockSpec((tm, D_MODEL), lambda i: (jnp.minimum(i, split_tiles - 1), 0)),
                     pl.BlockSpec((tm, D_MODEL), lambda i: (jnp.maximum(i - split_tiles, 0), 0))]
        out_shape = [jax.ShapeDtypeStruct((split_rows, D_MODEL), F32),
                     jax.ShapeDtypeStruct((t - split_rows, D_MODEL), F32)]
    kern = functools.partial(_combine_kernel, final_norm=final_norm, split_tiles=split_tiles)
    return pl.pallas_call(
        kern,
        grid=(nt,),
        in_specs=[
            pl.BlockSpec((tm * TOP_K,), lambda i: (i,), memory_space=pltpu.SMEM),
            pl.BlockSpec((tm * TOP_K,), lambda i: (jnp.minimum(i + 1, nt - 1),), memory_space=pltpu.SMEM),
            pl.BlockSpec((tm, LANES), lambda i: (i, 0)),
            pl.BlockSpec((tm, D_MODEL), lambda i: (i, 0)),
            pl.BlockSpec((1, D_MODEL), lambda i: (0, 0)),
            pl.BlockSpec(memory_space=pl.ANY),
        ],
        out_specs=out_specs,
        out_shape=out_shape,
        scratch_shapes=[pltpu.VMEM((2, TOP_K, tm, D_MODEL), F32), pltpu.SemaphoreType.DMA((2,))],
        compiler_params=_cparams("arbitrary"),
        name="combine",
    )(dest_flat, dest_flat, gates, x, gain.reshape(1, D_MODEL), ys)


def _sorted_rows_buffer(t):
    tm = ROW_TILE_EXPERT
    n_tiles = (t * TOP_K + N_EXPERTS * (tm - 1) + tm - 1) // tm
    return jnp.zeros((n_tiles * tm, D_MODEL), F32)


def _moe(x, h, logits, layer, w_up, b_up, w_down, b_down, gain, final_norm, xs_buf, split_rows=None):
    t = x.shape[0]
    tm = ROW_TILE_EXPERT
    eidx, gates, rank, counts = _route(logits)
    counts = counts[0, :N_EXPERTS].astype(I32)
    padded = (counts + tm - 1) // tm * tm
    pad_end = jnp.cumsum(padded)
    pad_start = pad_end - padded
    dest = (pad_start[eidx[:, :TOP_K]] + rank[:, :TOP_K]).reshape(-1)
    n_tiles = (t * TOP_K + N_EXPERTS * (tm - 1) + tm - 1) // tm
    tile_start = jnp.arange(n_tiles, dtype=I32) * tm
    blk_expert = jnp.minimum(jnp.sum((pad_end[None, :] <= tile_start[:, None]).astype(I32), axis=1), N_EXPERTS - 1)
    n_valid = (pad_end[-1:] // tm).astype(I32)
    xs = _scatter_rows(h, dest, xs_buf)
    ys = _experts(xs, blk_expert, n_valid, layer, w_up, b_up, w_down, b_down)
    return _combine(x, ys, dest, gates, gain, final_norm, split_rows), xs


def kernel(x_prompt, x_sample, cache_k, cache_v, state_hgrn, norm_mix, w_in, attn_sinks, lb_logits,
           hgrn_norm, w_out, norm_ffn, w_router, b_router, w_up, b_up, w_down, b_down, norm_final):
    bsz, seq, _ = x_prompt.shape
    n_seq, n_new, _ = x_sample.shape
    depth = w_in.shape[0]
    rows = cache_k.shape[2]
    tp = bsz * seq

    lb_soft = jax.nn.softmax(lb_logits.astype(F32), axis=0)
    lower_bounds = jnp.cumsum(lb_soft, axis=0) - lb_soft[0:1]

    cuts = np.cumsum([0, D_MODEL, KV_WIDTH, KV_WIDTH, D_MODEL, D_MODEL, D_MODEL, D_MODEL, D_MODEL, D_MODEL])
    part = lambda s: w_in[:, :, cuts[s]:cuts[s + 1]]
    w_in_p = jnp.concatenate([part(0), part(3), part(4), part(5), part(6), part(7), part(8), part(1), part(2)],
                             axis=-1).astype(BF16)
    w_out_b = w_out.astype(BF16)
    b_up_p = _split_pairs(b_up).reshape(depth, N_EXPERTS, 1, 2 * D_FF)
    b_down_r = b_down.reshape(depth, N_EXPERTS, 1, D_MODEL)
    w_router_p = jnp.pad(w_router, ((0, 0), (0, 0), (0, LANES - N_EXPERTS))).astype(BF16)
    b_router_p = jnp.pad(b_router, ((0, 0), (0, LANES - N_EXPERTS))).reshape(depth, 1, LANES)

    tab_p = _rope_tables(jnp.arange(seq, dtype=I32))
    tab_s = _rope_tables(PAST_LEN + jnp.arange(n_new, dtype=I32))

    x = jnp.concatenate([x_prompt.reshape(tp, D_MODEL), x_sample.reshape(n_seq * n_new, D_MODEL)], axis=0)
    kp_l, vp_l, sp_l, ks_l, vs_l, ss_l = [], [], [], [], [], []
    xs_buf = _sorted_rows_buffer(x.shape[0])
    for l in range(depth):
        z = _inproj(x, norm_mix[l], w_in_p[l])
        zs = z[tp:].reshape(n_seq, n_new, IN_DIM)
        a_p, kp, vp = _attn_prompt(z, attn_sinks[l], tab_p, bsz, seq)
        a_s, kx, vx = _attn_sample(zs, attn_sinks[l], tab_s,
                                   cache_k[l].reshape(n_seq, rows, KV_WIDTH),
                                   cache_v[l].reshape(n_seq, rows, KV_WIDTH))
        b_p, sp = _hgrn_prompt(z, lower_bounds[l], hgrn_norm[l], bsz, seq)
        b_s, sx = _hgrn_sample(zs, state_hgrn[l].astype(F32), lower_bounds[l], hgrn_norm[l])
        x_mid, h, logits = _merge(z, a_p, a_s.reshape(-1, D_MODEL), b_p, b_s.reshape(-1, D_MODEL), x,
                                  w_out_b[l], norm_ffn[l], w_router_p[l], b_router_p[l])
        last = l == depth - 1
        x, xs_buf = _moe(x_mid, h, logits, l, w_up, b_up_p[l], w_down, b_down_r[l], norm_final, last, xs_buf,
                         split_rows=tp if last else None)
        kp_l.append(kp.reshape(bsz, WINDOW, KV_HEADS, HEAD_DIM))
        vp_l.append(vp.reshape(bsz, WINDOW, KV_HEADS, HEAD_DIM))
        sp_l.append(jnp.swapaxes(sp, -1, -2))
        ks_l.append(kx.reshape(n_seq, rows, KV_HEADS, HEAD_DIM))
        vs_l.append(vx.reshape(n_seq, rows, KV_HEADS, HEAD_DIM))
        ss_l.append(sx)
    y_prompt = x[0].reshape(bsz, seq, D_MODEL)
    y_sample = x[1].reshape(n_seq, n_new, D_MODEL)
    return (y_prompt, y_sample, jnp.stack(kp_l), jnp.stack(vp_l), jnp.stack(sp_l).astype(x_prompt.dtype),
            jnp.stack(ks_l), jnp.stack(vs_l), jnp.stack(ss_l).astype(state_hgrn.dtype))
```
